```python
import math, functools
import jax, jax.numpy as jnp
from jax import lax
import numpy as np

D_MODEL = 2048
BATCH = 8
SEQ = 2048
DEPTH = 2

N_HEADS = 16
HEAD_DIM = D_MODEL // N_HEADS
NSA_KV_GROUPS = 4
NSA_Q_PER_GROUP = N_HEADS // NSA_KV_GROUPS
CMP_BLOCK = 32
CMP_STRIDE = 16
CMP_HIDDEN = 256
SEL_BLOCK = 64
SEL_TOP_N = 16
WINDOW = 512
ATTN_QBLOCK = 128
SEL_QBLOCK = 16
FOX_KV_HEADS = 4
FOX_Q_PER_KV = N_HEADS // FOX_KV_HEADS
N_EXPERTS = 32
N_EXPERT_GROUPS = 4
EXPERTS_PER_GROUP = N_EXPERTS // N_EXPERT_GROUPS
TOP_K = 2
D_EXPERT = 1024
MOE_BLOCK = 256
RMS_EPS = 1e-6
N_A_LAYERS = DEPTH // 2
N_B_LAYERS = DEPTH - N_A_LAYERS
NSA_KV_W = NSA_KV_GROUPS * HEAD_DIM
NSA_IN_W = D_MODEL + 6 * NSA_KV_W + 3 * N_HEADS
FOX_KV_W = FOX_KV_HEADS * HEAD_DIM
SHARED_IN_W = 2 * FOX_KV_W + N_HEADS

kernel_name = "yoco_nsa_fox_grouped_moe"


def _rmsnorm(x, g):
    xf = x.astype(jnp.float32)
    y = xf * lax.rsqrt(jnp.mean(xf * xf, axis=-1, keepdims=True) + RMS_EPS)
    return (y * g.astype(jnp.float32)).astype(x.dtype)


def _masked_softmax(s, mask, axis):
    s = jnp.where(mask, s.astype(jnp.float32), -jnp.inf)
    m = jnp.max(s, axis=axis, keepdims=True)
    m = jnp.where(jnp.isfinite(m), m, 0.0)
    e = jnp.exp(s - m)
    d = jnp.sum(e, axis=axis, keepdims=True)
    return e / jnp.where(d > 0, d, 1.0)


def _alibi_slopes():
    sl = 2.0 ** (-8.0 * np.arange(1, N_HEADS + 1) / N_HEADS)
    return jnp.asarray(sl, jnp.float32).reshape(NSA_KV_GROUPS, NSA_Q_PER_GROUP)


def _cmp_to_sel_matrix(n_cmp, n_sel):
    cs = np.arange(n_cmp)[:, None] * CMP_STRIDE
    ce = cs + CMP_BLOCK
    ss = np.arange(n_sel)[None, :] * SEL_BLOCK
    se = ss + SEL_BLOCK
    ov = np.clip(np.minimum(ce, se) - np.maximum(cs, ss), 0, None) / CMP_BLOCK
    return jnp.asarray(ov, jnp.float32)


def _compress(x, pe, w1, w2):
    B, T, G, hd = x.shape
    n_cmp = (T - CMP_BLOCK) // CMP_STRIDE + 1
    idx = np.arange(n_cmp)[:, None] * CMP_STRIDE + np.arange(CMP_BLOCK)[None, :]
    blocks = x[:, idx] + pe[None, None, :, None, :]
    flat = jnp.moveaxis(blocks, 3, 2).reshape(B, n_cmp, G, CMP_BLOCK * hd)
    return jax.nn.gelu(flat @ w1) @ w2


def _nsa(h, w_in, pe_k, pe_v, w1_k, w2_k, w1_v, w2_v, w_out, slopes):
    B, T, _ = h.shape
    G, R, hd = NSA_KV_GROUPS, NSA_Q_PER_GROUP, HEAD_DIM
    dt = h.dtype
    proj = h @ w_in
    cuts = [D_MODEL + i * NSA_KV_W for i in range(7)]
    q, kc, vc, ks, vs, kw, vw, gl = jnp.split(proj, cuts, axis=-1)
    q = q.reshape(B, T, G, R, hd) * (hd ** -0.5)
    kc, vc, ks, vs, kw, vw = [a.reshape(B, T, G, hd) for a in (kc, vc, ks, vs, kw, vw)]
    gates = jax.nn.sigmoid(gl.astype(jnp.float32)).reshape(B, T, G, R, 3).astype(dt)
    t_pos = jnp.arange(T)

    kcmp = _compress(kc, pe_k, w1_k, w2_k)
    vcmp = _compress(vc, pe_v, w1_v, w2_v)
    n_cmp = kcmp.shape[1]
    cmp_end = jnp.arange(n_cmp) * CMP_STRIDE + CMP_BLOCK - 1
    dist_c = t_pos[:, None] - cmp_end[None, :]
    s_c = jnp.einsum('btgrd,bngd->bgrtn', q, kcmp).astype(jnp.float32)
    s_c = s_c - slopes[:, :, None, None] * dist_c.astype(jnp.float32)
    p_cmp = _masked_softmax(s_c, dist_c >= 0, -1)
    o_cmp = jnp.einsum('bgrtn,bngd->btgrd', p_cmp.astype(dt), vcmp)

    n_sel = T // SEL_BLOCK
    imp = jnp.einsum('bgrtn,nj->bgtj', p_cmp, _cmp_to_sel_matrix(n_cmp, n_sel))
    cur = t_pos // SEL_BLOCK
    jb = jnp.arange(n_sel)
    forced = (jb[None, :] == 0) | (jb[None, :] == cur[:, None]) | (jb[None, :] == cur[:, None] - 1)
    imp = jnp.where(jb[None, :] > cur[:, None], -jnp.inf, jnp.where(forced, jnp.inf, imp))
    n_top = min(SEL_TOP_N, n_sel)
    _, sel_idx = lax.top_k(imp, n_top)

    ks_blk = ks.reshape(B, n_sel, SEL_BLOCK, G, hd).transpose(0, 3, 1, 2, 4)
    vs_blk = vs.reshape(B, n_sel, SEL_BLOCK, G, hd).transpose(0, 3, 1, 2, 4)
    nq = T // SEL_QBLOCK
    q_c = q.reshape(B, nq, SEL_QBLOCK, G, R, hd).transpose(1, 0, 2, 3, 4, 5)
    idx_c = sel_idx.reshape(B, G, nq, SEL_QBLOCK, n_top).transpose(2, 0, 1, 3, 4)
    t_c = t_pos.reshape(nq, SEL_QBLOCK)
    bi = jnp.arange(B)[:, None, None, None]
    gi = jnp.arange(G)[None, :, None, None]
    in_blk = jnp.arange(SEL_BLOCK)

    def sel_step(args):
        qb, ib, tb = args
        kg = ks_blk[bi, gi, ib]
        vg = vs_blk[bi, gi, ib]
        pos = ib[..., None] * SEL_BLOCK + in_blk
        dist = (tb[None, None, :, None, None] - pos)[:, :, None]
        s = jnp.einsum('bqgrd,bgqnld->bgrqnl', qb, kg).astype(jnp.float32)
        s = s - slopes[None, :, :, None, None, None] * dist.astype(jnp.float32)
        p = _masked_softmax(s, dist >= 0, (-2, -1))
        return jnp.einsum('bgrqnl,bgqnld->bqgrd', p.astype(dt), vg)

    o_slc = lax.map(sel_step, (q_c, idx_c, t_c))
    o_slc = o_slc.transpose(1, 0, 2, 3, 4, 5).reshape(B, T, G, R, hd)

    kw_p = jnp.pad(kw, ((0, 0), (WINDOW, 0), (0, 0), (0, 0)))
    vw_p = jnp.pad(vw, ((0, 0), (WINDOW, 0), (0, 0), (0, 0)))
    nb = T // ATTN_QBLOCK
    span = WINDOW + ATTN_QBLOCK
    q_w = q.reshape(B, nb, ATTN_QBLOCK, G, R, hd).transpose(1, 0, 2, 3, 4, 5)

    def win_step(args):
        qb, bidx = args
        q0 = bidx * ATTN_QBLOCK
        kb = lax.dynamic_slice_in_dim(kw_p, q0, span, axis=1)
        vb = lax.dynamic_slice_in_dim(vw_p, q0, span, axis=1)
        tq = q0 + jnp.arange(ATTN_QBLOCK)
        sk = q0 - WINDOW + jnp.arange(span)
        dist = tq[:, None] - sk[None, :]
        mask = (dist >= 0) & (dist < WINDOW) & (sk[None, :] >= 0)
        s = jnp.einsum('bqgrd,bkgd->bgrqk', qb, kb).astype(jnp.float32)
        s = s - slopes[:, :, None, None] * dist.astype(jnp.float32)
        p = _masked_softmax(s, mask, -1)
        return jnp.einsum('bgrqk,bkgd->bqgrd', p.astype(dt), vb)

    o_win = lax.map(win_step, (q_w, jnp.arange(nb)))
    o_win = o_win.transpose(1, 0, 2, 3, 4, 5).reshape(B, T, G, R, hd)

    o = gates[..., 0, None] * o_cmp + gates[..., 1, None] * o_slc + gates[..., 2, None] * o_win
    return o.reshape(B, T, D_MODEL) @ w_out


def _shared_kv(x, g_kv, w_kv, b_f):
    B, T, _ = x.shape
    h = _rmsnorm(x, g_kv)
    k, v, f = jnp.split(h @ w_kv, [FOX_KV_W, 2 * FOX_KV_W], axis=-1)
    logf = jax.nn.log_sigmoid(f.astype(jnp.float32) + b_f.astype(jnp.float32))
    c = jnp.cumsum(logf, axis=1)
    return (k.reshape(B, T, FOX_KV_HEADS, HEAD_DIM),
            v.reshape(B, T, FOX_KV_HEADS, HEAD_DIM), c)


def _fox(h, w_q, w_out, k_sh, v_sh, c_sh):
    B, T, _ = h.shape
    Gb, Rb, hd = FOX_KV_HEADS, FOX_Q_PER_KV, HEAD_DIM
    dt = h.dtype
    q = (h @ w_q).reshape(B, T, Gb, Rb, hd) * (hd ** -0.5)
    c = c_sh.reshape(B, T, Gb, Rb)
    c_key = c.transpose(0, 2, 3, 1)
    nb = T // ATTN_QBLOCK
    q_b = q.reshape(B, nb, ATTN_QBLOCK, Gb, Rb, hd).transpose(1, 0, 2, 3, 4, 5)
    c_b = c.reshape(B, nb, ATTN_QBLOCK, Gb, Rb).transpose(1, 0, 2, 3, 4)
    k_pos = jnp.arange(T)

    def step(args):
        qb, cb, bidx = args
        tq = bidx * ATTN_QBLOCK + jnp.arange(ATTN_QBLOCK)
        s = jnp.einsum('bqgrd,bkgd->bgrqk', qb, k_sh).astype(jnp.float32)
        s = s + cb.transpose(0, 2, 3, 1)[..., None] - c_key[:, :, :, None, :]
        p = _masked_softmax(s, tq[:, None] >= k_pos[None, :], -1)
        return jnp.einsum('bgrqk,bkgd->bqgrd', p.astype(dt), v_sh)

    o = lax.map(step, (q_b, c_b, jnp.arange(nb)))
    return o.transpose(1, 0, 2, 3, 4, 5).reshape(B, T, D_MODEL) @ w_out


def _route(h2, w_router, b_router):
    probs = jax.nn.softmax((h2 @ w_router).astype(jnp.float32), axis=-1)
    sel = probs + b_router.astype(jnp.float32)
    grp = sel.reshape(-1, N_EXPERT_GROUPS, EXPERTS_PER_GROUP)
    grp_score = jnp.sum(lax.top_k(grp, 2)[0], axis=-1)
    best_g = jnp.argmax(grp_score, axis=-1)
    in_group = (jnp.arange(N_EXPERTS) // EXPERTS_PER_GROUP)[None, :] == best_g[:, None]
    _, e_idx = lax.top_k(jnp.where(in_group, sel, -jnp.inf), TOP_K)
    w = jnp.take_along_axis(probs, e_idx, axis=-1)
    return e_idx, w / jnp.sum(w, axis=-1, keepdims=True)


def _moe(h, w_router, b_router, w_gate, w_up, w_down):
    B, T, D = h.shape
    h2 = h.reshape(B * T, D)
    n_tok = B * T
    n_asg = n_tok * TOP_K
    e_idx, e_w = _route(h2, w_router, b_router)
    flat_e = e_idx.reshape(-1)
    flat_tok = jnp.arange(n_asg, dtype=jnp.int32) // TOP_K
    order = jnp.argsort(flat_e)
    e_sorted = flat_e[order]
    tok_sorted = flat_tok[order]
    w_sorted = e_w.reshape(-1)[order]
    counts = jnp.bincount(flat_e, length=N_EXPERTS)
    padded = (counts + MOE_BLOCK - 1) // MOE_BLOCK * MOE_BLOCK
    pad_end = jnp.cumsum(padded)
    pad_start = pad_end - padded
    start = jnp.cumsum(counts) - counts
    dest = pad_start[e_sorted] + jnp.arange(n_asg) - start[e_sorted]
    n_blocks = -(-(n_asg + N_EXPERTS * (MOE_BLOCK - 1)) // MOE_BLOCK)
    buf_tok = jnp.zeros((n_blocks * MOE_BLOCK,), jnp.int32).at[dest].set(tok_sorted)
    block_e = jnp.minimum(
        jnp.searchsorted(pad_end, jnp.arange(n_blocks) * MOE_BLOCK, side='right'), N_EXPERTS - 1)
    xb = h2[buf_tok].reshape(n_blocks, MOE_BLOCK, D)

    def expert_block(args):
        xe, e = args
        return (jax.nn.silu(xe @ w_gate[e]) * (xe @ w_up[e])) @ w_down[e]

    yb = lax.map(expert_block, (xb, block_e)).reshape(-1, D)
    y = yb[dest] * w_sorted[:, None].astype(h.dtype)
    return jnp.zeros_like(h2).at[tok_sorted].add(y).reshape(B, T, D)


def setup_inputs(seed: int = 0) -> dict:
    key = jax.random.key(seed)
    ks = jax.random.split(key, 22)
    nrm = lambda k, shape, scale: jax.random.normal(k, shape, jnp.float32) * scale
    D, hd, E, F = D_MODEL, HEAD_DIM, N_EXPERTS, D_EXPERT
    out_scale = D ** -0.5 * (2 * DEPTH) ** -0.5
    return {
        "x": nrm(ks[0], (BATCH, SEQ, D), 1.0),
        "nsa_w_in": nrm(ks[1], (N_A_LAYERS, D, NSA_IN_W), D ** -0.5),
        "nsa_pe_k": nrm(ks[2], (N_A_LAYERS, CMP_BLOCK, hd), 0.1),
        "nsa_pe_v": nrm(ks[3], (N_A_LAYERS, CMP_BLOCK, hd), 0.1),
        "nsa_cmp_w1_k": nrm(ks[4], (N_A_LAYERS, CMP_BLOCK * hd, CMP_HIDDEN), (CMP_BLOCK * hd) ** -0.5),
        "nsa_cmp_w2_k": nrm(ks[5], (N_A_LAYERS, CMP_HIDDEN, hd), CMP_HIDDEN ** -0.5),
        "nsa_cmp_w1_v": nrm(ks[6], (N_A_LAYERS, CMP_BLOCK * hd, CMP_HIDDEN), (CMP_BLOCK * hd) ** -0.5),
        "nsa_cmp_w2_v": nrm(ks[7], (N_A_LAYERS, CMP_HIDDEN, hd), CMP_HIDDEN ** -0.5),
        "nsa_w_out": nrm(ks[8], (N_A_LAYERS, D, D), out_scale),
        "shared_norm": 1.0 + nrm(ks[9], (D,), 0.02),
        "shared_w_kv": nrm(ks[10], (D, SHARED_IN_W), D ** -0.5),
        "shared_b_f": 2.0 + nrm(ks[11], (N_HEADS,), 0.5),
        "fox_w_q": nrm(ks[12], (N_B_LAYERS, D, D), D ** -0.5),
        "fox_w_out": nrm(ks[13], (N_B_LAYERS, D, D), out_scale),
        "attn_norm": 1.0 + nrm(ks[14], (DEPTH, D), 0.02),
        "ffn_norm": 1.0 + nrm(ks[15], (DEPTH, D), 0.02),
        "router_w": nrm(ks[16], (D, E), D ** -0.5),
        "router_b": nrm(ks[17], (E,), 0.01),
        "moe_w_gate": nrm(ks[18], (DEPTH, E, D, F), D ** -0.5),
        "moe_w_up": nrm(ks[19], (DEPTH, E, D, F), D ** -0.5),
        "moe_w_down": nrm(ks[20], (DEPTH, E, F, D), F ** -0.5 * (2 * DEPTH) ** -0.5),
        "final_norm": 1.0 + nrm(ks[21], (D,), 0.02),
    }


def reference(x, nsa_w_in, nsa_pe_k, nsa_pe_v, nsa_cmp_w1_k, nsa_cmp_w2_k, nsa_cmp_w1_v,
              nsa_cmp_w2_v, nsa_w_out, shared_norm, shared_w_kv, shared_b_f, fox_w_q,
              fox_w_out, attn_norm, ffn_norm, router_w, router_b, moe_w_gate, moe_w_up,
              moe_w_down, final_norm):
    slopes = _alibi_slopes()
    k_sh = v_sh = c_sh = None
    for layer in range(DEPTH):
        h = _rmsnorm(x, attn_norm[layer])
        if layer < N_A_LAYERS:
            a = layer
            x = x + _nsa(h, nsa_w_in[a], nsa_pe_k[a], nsa_pe_v[a], nsa_cmp_w1_k[a],
                         nsa_cmp_w2_k[a], nsa_cmp_w1_v[a], nsa_cmp_w2_v[a], nsa_w_out[a], slopes)
        else:
            b = layer - N_A_LAYERS
            x = x + _fox(h, fox_w_q[b], fox_w_out[b], k_sh, v_sh, c_sh)
        h = _rmsnorm(x, ffn_norm[layer])
        x = x + _moe(h, router_w, router_b, moe_w_gate[layer], moe_w_up[layer], moe_w_down[layer])
        if layer == N_A_LAYERS - 1:
            k_sh, v_sh, c_sh = _shared_kv(x, shared_norm, shared_w_kv, shared_b_f)
    return _rmsnorm(x, final_norm)
```

```python
import functools

import numpy as np
import jax
import jax.numpy as jnp
from jax import lax
from jax.experimental import pallas as pl
from jax.experimental.pallas import tpu as pltpu

F32 = jnp.float32
BF16 = jnp.bfloat16
I32 = jnp.int32

D_MODEL = 2048
N_HEADS = 16
HEAD_DIM = 128
KV_GROUPS = 4
Q_PER_GROUP = N_HEADS // KV_GROUPS
CMP_BLOCK = 32
CMP_STRIDE = 16
CMP_HIDDEN = 256
SEL_BLOCK = 64
SEL_TOP_N = 16
WINDOW = 512
N_EXPERTS = 32
N_EXPERT_GROUPS = 4
EXPERTS_PER_GROUP = N_EXPERTS // N_EXPERT_GROUPS
TOP_K = 2
D_EXPERT = 1024
RMS_EPS = 1e-6
N_GATES = 3 * N_HEADS

LANES = 128
VMEM_LIMIT = 48 * 1024 * 1024
ROW_TILE = 512
COL_TILE = 512
NORM_CHUNK = 128
Q_TILE = 128
KV_TILE = 128
MOE_ROWS = 256
MOE_F_CHUNK = 512
MASK_VALUE = -1e30

_NT = (((1,), (1,)), ((), ()))


def _params(sem):
    return pltpu.CompilerParams(dimension_semantics=sem, vmem_limit_bytes=VMEM_LIMIT)


def _split3(x):
    a = x.astype(BF16)
    r = x - a.astype(F32)
    b = r.astype(BF16)
    c = (r - b.astype(F32)).astype(BF16)
    return a, b, c


def _act(kind, v):
    if kind == "sigmoid":
        return jax.nn.sigmoid(v)
    if kind == "log_sigmoid":
        return jnp.minimum(v, 0.0) - jnp.log1p(jnp.exp(-jnp.abs(v)))
    raise ValueError(kind)


def _norm_proj_kernel(*refs, n_scaled, scale, extra):
    if extra:
        x_ref, g_ref, w_ref, wx_ref, bx_ref, o_ref, ox_ref, h_scr = refs
    else:
        x_ref, g_ref, w_ref, o_ref, h_scr = refs
    j = pl.program_id(1)
    tm = x_ref.shape[0]

    @pl.when(j == 0)
    def _():
        def body(c, carry):
            r0 = pl.multiple_of(c * NORM_CHUNK, NORM_CHUNK)
            x = x_ref[pl.ds(r0, NORM_CHUNK), :]
            ms = jnp.mean(x * x, axis=-1, keepdims=True)
            hb = ((x * lax.rsqrt(ms + RMS_EPS)) * g_ref[...]).astype(BF16)
            h_scr[pl.ds(r0, NORM_CHUNK), :] = hb
            if extra:
                e = jnp.dot(hb, wx_ref[...], preferred_element_type=F32) + bx_ref[...]
                ox_ref[pl.ds(r0, NORM_CHUNK), :] = _act(extra, e)
            return carry
        lax.fori_loop(0, tm // NORM_CHUNK, body, 0)

    acc = jnp.dot(h_scr[...], w_ref[...], preferred_element_type=F32)
    if n_scaled:
        acc = acc * jnp.where(j < n_scaled, jnp.float32(scale), jnp.float32(1.0))
    for c in range(o_ref.shape[0]):
        o_ref[c] = acc[:, c * LANES:(c + 1) * LANES].astype(o_ref.dtype)


def _norm_proj(x2, gain, w, *, n_scaled_cols=0, scale=1.0, wx=None, bx=None, extra=None):
    n, d = x2.shape
    cols = w.shape[1]
    assert n % ROW_TILE == 0 and cols % COL_TILE == 0
    grid = (n // ROW_TILE, cols // COL_TILE)
    in_specs = [
        pl.BlockSpec((ROW_TILE, d), lambda i, j: (i, 0)),
        pl.BlockSpec((1, d), lambda i, j: (0, 0)),
        pl.BlockSpec((d, COL_TILE), lambda i, j: (0, j)),
    ]
    args = [x2, gain.reshape(1, d).astype(F32), w.astype(BF16)]
    out_shape = [jax.ShapeDtypeStruct((cols // LANES, n, LANES), BF16)]
    out_specs = [pl.BlockSpec((COL_TILE // LANES, ROW_TILE, LANES), lambda i, j: (j, i, 0))]
    if extra:
        in_specs += [pl.BlockSpec((d, LANES), lambda i, j: (0, 0)),
                     pl.BlockSpec((1, LANES), lambda i, j: (0, 0))]
        pad = LANES - wx.shape[1]
        args += [jnp.pad(wx, ((0, 0), (0, pad))).astype(BF16),
                 jnp.pad(bx.astype(F32), (0, pad)).reshape(1, LANES)]
        out_shape.append(jax.ShapeDtypeStruct((n, LANES), F32))
        out_specs.append(pl.BlockSpec((ROW_TILE, LANES), lambda i, j: (i, 0)))
    kern = functools.partial(_norm_proj_kernel, n_scaled=n_scaled_cols // COL_TILE,
                             scale=scale, extra=extra)
    res = pl.pallas_call(
        kern, grid=grid, in_specs=in_specs, out_specs=out_specs, out_shape=out_shape,
        scratch_shapes=[pltpu.VMEM((ROW_TILE, d), BF16)],
        compiler_params=_params(("parallel", "arbitrary")), name="norm_proj",
    )(*args)
    return res if extra else res[0]


def _gelu_tanh(v):
    c = np.sqrt(2.0 / np.pi).astype(np.float32)
    return 0.5 * v * (1.0 + jnp.tanh(c * (v + 0.044715 * (v * v * v))))


def _compress_kernel(c_ref, pe_ref, w1_ref, w2_ref, o_ref):
    half = w1_ref.shape[0] // 2
    ch = c_ref[...].astype(F32)
    top = (ch + pe_ref[0:1, :]).astype(BF16)
    bot = (ch + pe_ref[1:2, :]).astype(BF16)
    a = jnp.dot(top, w1_ref[0:half, :], preferred_element_type=F32)
    b = jnp.dot(bot, w1_ref[half:, :], preferred_element_type=F32)
    n = a.shape[0]
    b_next = jnp.concatenate([b[1:, :], jnp.zeros((1, b.shape[1]), F32)], axis=0)
    hid = _gelu_tanh(a + b_next).astype(BF16)
    out = jnp.dot(hid, w2_ref[...], preferred_element_type=F32)
    row = lax.broadcasted_iota(I32, out.shape, 0)
    o_ref[...] = jnp.where(row < n - 1, out, 0.0).astype(o_ref.dtype)


def _compress(proj, head0, pe, w1, w2, batch, seq):
    chunks = seq // CMP_STRIDE
    width = CMP_STRIDE * HEAD_DIM
    view = proj.reshape(proj.shape[0], batch, chunks, width)
    pe2 = pe.astype(F32).reshape(2, width)
    return pl.pallas_call(
        _compress_kernel, grid=(batch, KV_GROUPS),
        in_specs=[
            pl.BlockSpec((None, None, chunks, width), lambda b, g: (head0 + g, b, 0, 0)),
            pl.BlockSpec((2, width), lambda b, g: (0, 0)),
            pl.BlockSpec((2 * width, CMP_HIDDEN), lambda b, g: (0, 0)),
            pl.BlockSpec((CMP_HIDDEN, HEAD_DIM), lambda b, g: (0, 0)),
        ],
        out_specs=pl.BlockSpec((None, None, chunks, HEAD_DIM), lambda b, g: (b, g, 0, 0)),
        out_shape=jax.ShapeDtypeStruct((batch, KV_GROUPS, chunks, HEAD_DIM), BF16),
        compiler_params=_params(("parallel", "parallel")), name="nsa_compress",
    )(view, pe2, w1.astype(BF16), w2.astype(BF16))


def _flash_init(m_scr, l_scr, acc_scr):
    m_scr[...] = jnp.full(m_scr.shape, MASK_VALUE, F32)
    l_scr[...] = jnp.zeros(l_scr.shape, F32)
    acc_scr[...] = jnp.zeros(acc_scr.shape, F32)


def _flash_step(q, k_blk, v_blk, bias, m_scr, l_scr, acc_scr):
    s = lax.dot_general(q, k_blk, _NT, preferred_element_type=F32) + bias
    m_prev = m_scr[...]
    m_new = jnp.maximum(m_prev, jnp.max(s, axis=-1, keepdims=True))
    alpha = jnp.exp(m_prev - m_new)
    p = jnp.exp(s - m_new)
    l_scr[...] = alpha * l_scr[...] + jnp.sum(p, axis=-1, keepdims=True)
    acc_scr[...] = alpha * acc_scr[...] + jnp.dot(p.astype(BF16), v_blk,
                                                  preferred_element_type=F32)
    m_scr[...] = m_new


def _lane_pick(tile, lane):
    idx = lax.broadcasted_iota(I32, tile.shape, 1)
    return jnp.sum(jnp.where(idx == lane, tile, 0.0), axis=-1, keepdims=True)


def _nsa_attn_kernel(slopes_ref, q_ref, kc_ref, vc_ref, ks_ref, vs_ref, kw_ref, vw_ref,
                     gate_ref, ovl_ref, exp_ref, o_ref, m_scr, l_scr, acc_scr, *, n_cmp):
    g = pl.program_id(1)
    i = pl.program_id(2)
    q0 = i * Q_TILE
    rows = Q_PER_GROUP * Q_TILE
    q = q_ref[...].reshape(rows, HEAD_DIM)
    slopes = [slopes_ref[g * Q_PER_GROUP + r] for r in range(Q_PER_GROUP)]
    t_loc = lax.broadcasted_iota(I32, (Q_TILE, LANES), 0)
    c_loc = lax.broadcasted_iota(I32, (Q_TILE, LANES), 1)

    s = lax.dot_general(q, kc_ref[...], _NT, preferred_element_type=F32)
    dist_c = q0 + t_loc - (c_loc * CMP_STRIDE + (CMP_BLOCK - 1))
    valid_c = jnp.logical_and(dist_c >= 0, c_loc < n_cmp)
    dist_cf = dist_c.astype(F32)
    probs = []
    for r in range(Q_PER_GROUP):
        s_r = s[r * Q_TILE:(r + 1) * Q_TILE, :] - slopes[r] * dist_cf
        s_r = jnp.where(valid_c, s_r, -jnp.inf)
        m = jnp.max(s_r, axis=-1, keepdims=True)
        m = jnp.where(m == -jnp.inf, 0.0, m)
        e = jnp.exp(s_r - m)
        d = jnp.sum(e, axis=-1, keepdims=True)
        probs.append(e / jnp.where(d > 0, d, 1.0))
    p_all = jnp.concatenate(probs, axis=0)
    o_cmp = jnp.dot(p_all.astype(BF16), vc_ref[...], preferred_element_type=F32)

    p_sum = probs[0] + probs[1] + probs[2] + probs[3]
    imp = sum(jnp.dot(piece, ovl_ref[...], preferred_element_type=F32)
              for piece in _split3(p_sum))
    cur = (q0 + t_loc) // SEL_BLOCK
    forced = jnp.logical_or(c_loc == 0, jnp.logical_or(c_loc == cur, c_loc == cur - 1))
    imp = jnp.where(c_loc > cur, -jnp.inf, jnp.where(forced, jnp.inf, imp))
    n_sel = exp_ref.shape[0] * (KV_TILE // SEL_BLOCK)
    rank = jnp.zeros((Q_TILE, LANES), F32)
    for k in range(n_sel):
        col = imp[:, k:k + 1]
        ahead = jnp.logical_or(col > imp, jnp.logical_and(col == imp, c_loc > k))
        rank = rank + jnp.where(ahead, 1.0, 0.0)
    chosen = jnp.where(rank < SEL_TOP_N, 1.0, 0.0).astype(BF16)

    def sweep(k_ref, v_ref, lo, hi, bias_fn):
        _flash_init(m_scr, l_scr, acc_scr)

        def body(kb, carry):
            k0 = pl.multiple_of(kb * KV_TILE, KV_TILE)
            dist = t_loc - c_loc + (q0 - k0)
            _flash_step(q, k_ref[pl.ds(k0, KV_TILE), :], v_ref[pl.ds(k0, KV_TILE), :],
                        bias_fn(kb, dist), m_scr, l_scr, acc_scr)
            return carry
        lax.fori_loop(lo, hi, body, 0)
        return acc_scr[...] / l_scr[...]

    def alibi(valid, dist_f):
        return jnp.concatenate(
            [jnp.where(valid, -slopes[r] * dist_f, MASK_VALUE) for r in range(Q_PER_GROUP)],
            axis=0)

    def sel_bias(kb, dist):
        picked = jnp.dot(chosen, exp_ref[kb], preferred_element_type=F32)
        return alibi(jnp.logical_and(picked > 0.5, dist >= 0), dist.astype(F32))

    def win_bias(kb, dist):
        return alibi(jnp.logical_and(dist >= 0, dist < WINDOW), dist.astype(F32))

    o_slc = sweep(ks_ref, vs_ref, 0, i + 1, sel_bias)
    o_win = sweep(kw_ref, vw_ref, jnp.maximum(i - WINDOW // KV_TILE, 0), i + 1, win_bias)

    gates = gate_ref[...]
    for r in range(Q_PER_GROUP):
        sl = slice(r * Q_TILE, (r + 1) * Q_TILE)
        base = (g * Q_PER_GROUP + r) * 3
        o_r = (_lane_pick(gates, base) * o_cmp[sl] + _lane_pick(gates, base + 1) * o_slc[sl]
               + _lane_pick(gates, base + 2) * o_win[sl])
        o_ref[:, r * HEAD_DIM:(r + 1) * HEAD_DIM] = o_r.astype(o_ref.dtype)


def _cmp_to_sel_overlap(n_cmp, n_sel):
    cs = np.arange(LANES)[:, None] * CMP_STRIDE
    ce = cs + CMP_BLOCK
    ss = np.arange(LANES)[None, :] * SEL_BLOCK
    se = ss + SEL_BLOCK
    ov = np.clip(np.minimum(ce, se) - np.maximum(cs, ss), 0, None) / CMP_BLOCK
    ov = ov * (np.arange(LANES)[:, None] < n_cmp) * (np.arange(LANES)[None, :] < n_sel)
    return jnp.asarray(ov, BF16)


def _sel_block_expander(seq):
    n_kv = seq // KV_TILE
    blk = (np.arange(n_kv)[:, None, None] * KV_TILE + np.arange(KV_TILE)[None, None, :]) // SEL_BLOCK
    return jnp.asarray(blk == np.arange(LANES)[None, :, None], BF16)


def _alibi_slopes():
    sl = 2.0 ** (-8.0 * np.arange(1, N_HEADS + 1) / N_HEADS)
    return jnp.asarray(sl, F32)


def _nsa_attention(proj, kcmp, vcmp, gates, batch, seq):
    n = batch * seq
    nq = seq // Q_TILE
    n_cmp = (seq - CMP_BLOCK) // CMP_STRIDE + 1
    n_sel = seq // SEL_BLOCK
    assert seq // CMP_STRIDE == LANES and n_sel <= LANES
    kv_heads = [N_HEADS + 2 * KV_GROUPS + k * KV_GROUPS for k in range(4)]

    def kv_spec(h0):
        return pl.BlockSpec((None, seq, HEAD_DIM), lambda b, g, i: (h0 + g, b, 0))

    cmp_spec = pl.BlockSpec((None, None, LANES, HEAD_DIM), lambda b, g, i: (b, g, 0, 0))
    return pl.pallas_call(
        functools.partial(_nsa_attn_kernel, n_cmp=n_cmp),
        grid=(batch, KV_GROUPS, nq),
        in_specs=[
            pl.BlockSpec(memory_space=pltpu.SMEM),
            pl.BlockSpec((Q_PER_GROUP, Q_TILE, HEAD_DIM), lambda b, g, i: (g, b * nq + i, 0)),
            cmp_spec, cmp_spec,
            kv_spec(kv_heads[0]), kv_spec(kv_heads[1]), kv_spec(kv_heads[2]), kv_spec(kv_heads[3]),
            pl.BlockSpec((Q_TILE, LANES), lambda b, g, i: (b * nq + i, 0)),
            pl.BlockSpec((LANES, LANES), lambda b, g, i: (0, 0)),
            pl.BlockSpec((seq // KV_TILE, LANES, KV_TILE), lambda b, g, i: (0, 0, 0)),
        ],
        out_specs=pl.BlockSpec((Q_TILE, Q_PER_GROUP * HEAD_DIM), lambda b, g, i: (b * nq + i, g)),
        out_shape=jax.ShapeDtypeStruct((n, D_MODEL), BF16),
        scratch_shapes=[pltpu.VMEM((Q_PER_GROUP * Q_TILE, 1), F32),
                        pltpu.VMEM((Q_PER_GROUP * Q_TILE, 1), F32),
                        pltpu.VMEM((Q_PER_GROUP * Q_TILE, HEAD_DIM), F32)],
        compiler_params=_params(("parallel", "parallel", "arbitrary")), name="nsa_attention",
    )(_alibi_slopes(), proj, kcmp, vcmp, proj, proj, proj, proj, gates,
      _cmp_to_sel_overlap(n_cmp, n_sel), _sel_block_expander(seq))


def _fox_attn_kernel(q_ref, k_ref, v_ref, cq_ref, ck_ref, o_ref, m_scr, l_scr, acc_scr):
    g = pl.program_id(1)
    i = pl.program_id(2)
    q0 = i * Q_TILE
    q = q_ref[...].reshape(Q_PER_GROUP * Q_TILE, HEAD_DIM)
    t_loc = lax.broadcasted_iota(I32, (Q_TILE, LANES), 0)
    c_loc = lax.broadcasted_iota(I32, (Q_TILE, LANES), 1)
    c_tile = cq_ref[...]
    c_q = [_lane_pick(c_tile, g * Q_PER_GROUP + r) for r in range(Q_PER_GROUP)]
    _flash_init(m_scr, l_scr, acc_scr)

    def body(kb, carry):
        k0 = pl.multiple_of(kb * KV_TILE, KV_TILE)
        causal = (t_loc - c_loc + (q0 - k0)) >= 0
        bias = jnp.concatenate(
            [jnp.where(causal, c_q[r] - ck_ref[r, pl.ds(kb, 1), :], MASK_VALUE)
             for r in range(Q_PER_GROUP)], axis=0)
        _flash_step(q, k_ref[pl.ds(k0, KV_TILE), :], v_ref[pl.ds(k0, KV_TILE), :], bias,
                    m_scr, l_scr, acc_scr)
        return carry
    lax.fori_loop(0, i + 1, body, 0)
    out = acc_scr[...] / l_scr[...]
    for r in range(Q_PER_GROUP):
        o_ref[:, r * HEAD_DIM:(r + 1) * HEAD_DIM] = out[r * Q_TILE:(r + 1) * Q_TILE].astype(o_ref.dtype)


def _fox_attention(q_hm, kv_hm, c_tok, c_rows, batch, seq):
    n = batch * seq
    nq = seq // Q_TILE
    return pl.pallas_call(
        _fox_attn_kernel, grid=(batch, KV_GROUPS, nq),
        in_specs=[
            pl.BlockSpec((Q_PER_GROUP, Q_TILE, HEAD_DIM), lambda b, g, i: (g, b * nq + i, 0)),
            pl.BlockSpec((None, seq, HEAD_DIM), lambda b, g, i: (g, b, 0)),
            pl.BlockSpec((None, seq, HEAD_DIM), lambda b, g, i: (KV_GROUPS + g, b, 0)),
            pl.BlockSpec((Q_TILE, LANES), lambda b, g, i: (b * nq + i, 0)),
            pl.BlockSpec((None, Q_PER_GROUP, seq // KV_TILE, KV_TILE), lambda b, g, i: (b, g, 0, 0)),
        ],
        out_specs=pl.BlockSpec((Q_TILE, Q_PER_GROUP * HEAD_DIM), lambda b, g, i: (b * nq + i, g)),
        out_shape=jax.ShapeDtypeStruct((n, D_MODEL), BF16),
        scratch_shapes=[pltpu.VMEM((Q_PER_GROUP * Q_TILE, 1), F32),
                        pltpu.VMEM((Q_PER_GROUP * Q_TILE, 1), F32),
                        pltpu.VMEM((Q_PER_GROUP * Q_TILE, HEAD_DIM), F32)],
        compiler_params=_params(("parallel", "parallel", "arbitrary")), name="fox_attention",
    )(q_hm, kv_hm, kv_hm, c_tok, c_rows)


def _cumsum_kernel(x_ref, tri_ref, o_ref, carry_scr):
    @pl.when(pl.program_id(1) == 0)
    def _():
        carry_scr[...] = jnp.zeros(carry_scr.shape, F32)
    tri = tri_ref[...]
    c = sum(jnp.dot(tri, piece, preferred_element_type=F32) for piece in _split3(x_ref[...]))
    c = c + carry_scr[...]
    o_ref[...] = c
    carry_scr[...] = c[ROW_TILE - 1:ROW_TILE, :]


def _cumsum_seq(x2, batch, seq):
    nt = seq // ROW_TILE
    tri = jnp.asarray(np.tril(np.ones((ROW_TILE, ROW_TILE))), BF16)
    return pl.pallas_call(
        _cumsum_kernel, grid=(batch, nt),
        in_specs=[pl.BlockSpec((ROW_TILE, LANES), lambda b, t: (b * nt + t, 0)),
                  pl.BlockSpec((ROW_TILE, ROW_TILE), lambda b, t: (0, 0))],
        out_specs=pl.BlockSpec((ROW_TILE, LANES), lambda b, t: (b * nt + t, 0)),
        out_shape=jax.ShapeDtypeStruct(x2.shape, F32),
        scratch_shapes=[pltpu.VMEM((1, LANES), F32)],
        compiler_params=_params(("parallel", "arbitrary")), name="cumsum_seq",
    )(x2, tri)


def _proj_residual_kernel(o_ref, w_ref, x_ref, out_ref):
    out_ref[...] = x_ref[...] + jnp.dot(o_ref[...], w_ref[...], preferred_element_type=F32)


def _proj_residual(o2, w, x2):
    n, d = x2.shape
    return pl.pallas_call(
        _proj_residual_kernel, grid=(n // ROW_TILE, d // COL_TILE),
        in_specs=[pl.BlockSpec((ROW_TILE, o2.shape[1]), lambda i, j: (i, 0)),
                  pl.BlockSpec((o2.shape[1], COL_TILE), lambda i, j: (0, j)),
                  pl.BlockSpec((ROW_TILE, COL_TILE), lambda i, j: (i, j))],
        out_specs=pl.BlockSpec((ROW_TILE, COL_TILE), lambda i, j: (i, j)),
        out_shape=jax.ShapeDtypeStruct((n, d), F32),
        compiler_params=_params(("parallel", "parallel")), name="proj_residual",
    )(o2, w.astype(BF16), x2)


def _top2_rows(v):
    idx = lax.broadcasted_iota(I32, v.shape, 0).astype(F32)
    m1 = jnp.max(v, axis=0, keepdims=True)
    i1 = jnp.min(jnp.where(v == m1, idx, EXPERTS_PER_GROUP), axis=0, keepdims=True)
    v2 = jnp.where(idx == i1, -jnp.inf, v)
    m2 = jnp.max(v2, axis=0, keepdims=True)
    i2 = jnp.min(jnp.where(v2 == m2, idx, EXPERTS_PER_GROUP), axis=0, keepdims=True)
    return m1, i1, m2, i2


def _router_kernel(x_ref, g_ref, wr_ref, br_ref, h_ref, e_ref, p_ref):
    x = x_ref[...]
    ms = jnp.mean(x * x, axis=-1, keepdims=True)
    h = (x * lax.rsqrt(ms + RMS_EPS)) * g_ref[...]
    h_ref[...] = h.astype(h_ref.dtype)
    h1, h2, _ = _split3(h)
    w1, w2, _ = _split3(wr_ref[...])
    logits = None
    for a, b in ((w1, h1), (w1, h2), (w2, h1)):
        part = lax.dot_general(a, b, _NT, preferred_element_type=F32)
        logits = part if logits is None else logits + part
    m = jnp.max(logits, axis=0, keepdims=True)
    e = jnp.exp(logits - m)
    probs = e / jnp.sum(e, axis=0, keepdims=True)
    sel = probs + br_ref[...]
    idx = lax.broadcasted_iota(I32, (EXPERTS_PER_GROUP, x.shape[0]), 0).astype(F32)
    best_score = best_e1 = best_e2 = best_p1 = best_p2 = None
    for k in range(N_EXPERT_GROUPS):
        rows = slice(k * EXPERTS_PER_GROUP, (k + 1) * EXPERTS_PER_GROUP)
        m1, i1, m2, i2 = _top2_rows(sel[rows])
        pk = probs[rows]
        p1 = jnp.sum(jnp.where(idx == i1, pk, 0.0), axis=0, keepdims=True)
        p2 = jnp.sum(jnp.where(idx == i2, pk, 0.0), axis=0, keepdims=True)
        score = m1 + m2
        e1 = i1 + k * EXPERTS_PER_GROUP
        e2 = i2 + k * EXPERTS_PER_GROUP
        if k == 0:
            best_score, best_e1, best_e2, best_p1, best_p2 = score, e1, e2, p1, p2
        else:
            take = score > best_score
            best_score = jnp.where(take, score, best_score)
            best_e1 = jnp.where(take, e1, best_e1)
            best_e2 = jnp.where(take, e2, best_e2)
            best_p1 = jnp.where(take, p1, best_p1)
            best_p2 = jnp.where(take, p2, best_p2)
    tot = best_p1 + best_p2
    e_ref[...] = jnp.concatenate([best_e1, best_e2], axis=0).astype(I32)
    p_ref[...] = jnp.concatenate([best_p1 / tot, best_p2 / tot], axis=0)


def _router(x2, gain, w_router, b_router):
    n, d = x2.shape
    tm = ROW_TILE
    return pl.pallas_call(
        _router_kernel, grid=(n // tm,),
        in_specs=[pl.BlockSpec((tm, d), lambda i: (i, 0)),
                  pl.BlockSpec((1, d), lambda i: (0, 0)),
                  pl.BlockSpec((N_EXPERTS, d), lambda i: (0, 0)),
                  pl.BlockSpec((N_EXPERTS, 1), lambda i: (0, 0))],
        out_specs=[pl.BlockSpec((tm, d), lambda i: (i, 0)),
                   pl.BlockSpec((TOP_K, tm), lambda i: (0, i)),
                   pl.BlockSpec((TOP_K, tm), lambda i: (0, i))],
        out_shape=[jax.ShapeDtypeStruct((n, d), BF16),
                   jax.ShapeDtypeStruct((TOP_K, n), I32),
                   jax.ShapeDtypeStruct((TOP_K, n), F32)],
        compiler_params=_params(("parallel",)), name="router",
    )(x2, gain.reshape(1, d).astype(F32), w_router.T.astype(F32),
      b_router.astype(F32).reshape(N_EXPERTS, 1))


def _expert_kernel(be_ref, nu_ref, x_ref, wg_ref, wu_ref, wd_ref, o_ref):
    i = pl.program_id(0)

    @pl.when(i < nu_ref[0])
    def _():
        x = x_ref[...]
        for c in range(D_EXPERT // MOE_F_CHUNK):
            cols = slice(c * MOE_F_CHUNK, (c + 1) * MOE_F_CHUNK)
            gt = jnp.dot(x, wg_ref[:, cols], preferred_element_type=F32)
            up = jnp.dot(x, wu_ref[:, cols], preferred_element_type=F32)
            act = (gt * jax.nn.sigmoid(gt) * up).astype(BF16)
            y = jnp.dot(act, wd_ref[cols, :], preferred_element_type=F32)
            if c == 0:
                o_ref[...] = y
            else:
                o_ref[...] += y

    @pl.when(i >= nu_ref[0])
    def _():
        o_ref[...] = jnp.zeros(o_ref.shape, o_ref.dtype)


def _expert_mlp(xb, block_e, n_used, w_gate, w_up, w_down):
    rows, d = xb.shape
    n_blocks = rows // MOE_ROWS
    f = w_gate.shape[2]
    grid_spec = pltpu.PrefetchScalarGridSpec(
        num_scalar_prefetch=2, grid=(n_blocks,),
        in_specs=[pl.BlockSpec((MOE_ROWS, d), lambda i, be, nu: (i, 0)),
                  pl.BlockSpec((None, d, f), lambda i, be, nu: (be[i], 0, 0)),
                  pl.BlockSpec((None, d, f), lambda i, be, nu: (be[i], 0, 0)),
                  pl.BlockSpec((None, f, d), lambda i, be, nu: (be[i], 0, 0))],
        out_specs=pl.BlockSpec((MOE_ROWS, d), lambda i, be, nu: (i, 0)),
    )
    return pl.pallas_call(
        _expert_kernel, grid_spec=grid_spec,
        out_shape=jax.ShapeDtypeStruct((rows, d), F32),
        compiler_params=_params(("arbitrary",)), name="expert_mlp",
    )(block_e, n_used, xb, w_gate, w_up, w_down)


def _moe(x2, gain, w_router, b_router, w_gate, w_up, w_down):
    n, d = x2.shape
    h, e_idx, e_w = _router(x2, gain, w_router, b_router)
    flat_e = e_idx.T.reshape(-1)
    n_asg = n * TOP_K
    onehot = (flat_e[:, None] == jnp.arange(N_EXPERTS, dtype=I32)[None, :]).astype(I32)
    counts = jnp.sum(onehot, axis=0)
    rank = jnp.sum((jnp.cumsum(onehot, axis=0) - onehot) * onehot, axis=1)
    blocks_per_e = (counts + MOE_ROWS - 1) // MOE_ROWS
    blk_end = jnp.cumsum(blocks_per_e)
    row_start = (blk_end - blocks_per_e) * MOE_ROWS
    dest = row_start[flat_e] + rank
    n_blocks = -(-(n_asg + N_EXPERTS * (MOE_ROWS - 1)) // MOE_ROWS)
    slot_tok = jnp.zeros((n_blocks * MOE_ROWS,), I32).at[dest].set(
        jnp.arange(n_asg, dtype=I32) // TOP_K)
    n_used = blk_end[-1:].astype(I32)
    blk = jnp.minimum(jnp.arange(n_blocks, dtype=I32), n_used[0] - 1)
    block_e = jnp.sum((blk_end[None, :] <= blk[:, None]).astype(I32), axis=1)
    block_e = jnp.minimum(block_e, N_EXPERTS - 1)
    xb = h[slot_tok]
    yb = _expert_mlp(xb, block_e, n_used, w_gate.astype(BF16), w_up.astype(BF16),
                     w_down.astype(BF16))
    y = yb[dest].reshape(n, TOP_K, d) * e_w.T[:, :, None]
    return x2 + jnp.sum(y, axis=1)


def _rmsnorm_kernel(x_ref, g_ref, o_ref):
    x = x_ref[...]
    ms = jnp.mean(x * x, axis=-1, keepdims=True)
    o_ref[...] = (x * lax.rsqrt(ms + RMS_EPS)) * g_ref[...]


def _rmsnorm(x2, gain):
    n, d = x2.shape
    tm = NORM_CHUNK * 2
    return pl.pallas_call(
        _rmsnorm_kernel, grid=(n // tm,),
        in_specs=[pl.BlockSpec((tm, d), lambda i: (i, 0)), pl.BlockSpec((1, d), lambda i: (0, 0))],
        out_specs=pl.BlockSpec((tm, d), lambda i: (i, 0)),
        out_shape=jax.ShapeDtypeStruct((n, d), F32),
        compiler_params=_params(("parallel",)), name="final_norm",
    )(x2, gain.reshape(1, d).astype(F32))


def _nsa_layer(x2, gain, w_in, pe_k, pe_v, w1_k, w2_k, w1_v, w2_v, w_out, batch, seq):
    main_cols = D_MODEL + 6 * KV_GROUPS * HEAD_DIM
    proj, gates = _norm_proj(
        x2, gain, w_in[:, :main_cols], n_scaled_cols=D_MODEL, scale=HEAD_DIM ** -0.5,
        wx=w_in[:, main_cols:], bx=jnp.zeros((N_GATES,), F32), extra="sigmoid")
    kcmp = _compress(proj, N_HEADS, pe_k, w1_k, w2_k, batch, seq)
    vcmp = _compress(proj, N_HEADS + KV_GROUPS, pe_v, w1_v, w2_v, batch, seq)
    o = _nsa_attention(proj, kcmp, vcmp, gates, batch, seq)
    return _proj_residual(o, w_out, x2)


def _shared_kv(x2, gain, w_kv, b_f, batch, seq):
    kv_cols = 2 * KV_GROUPS * HEAD_DIM
    kv_hm, logf = _norm_proj(x2, gain, w_kv[:, :kv_cols], wx=w_kv[:, kv_cols:], bx=b_f,
                             extra="log_sigmoid")
    c_tok = _cumsum_seq(logf, batch, seq)
    c_rows = c_tok[:, :N_HEADS].reshape(batch, seq, N_HEADS).transpose(0, 2, 1)
    c_rows = c_rows.reshape(batch, N_HEADS, seq // KV_TILE, KV_TILE)
    return kv_hm, c_tok, c_rows


def _fox_layer(x2, gain, w_q, w_out, shared, batch, seq):
    kv_hm, c_tok, c_rows = shared
    q_hm = _norm_proj(x2, gain, w_q, n_scaled_cols=D_MODEL, scale=HEAD_DIM ** -0.5)
    o = _fox_attention(q_hm, kv_hm, c_tok, c_rows, batch, seq)
    return _proj_residual(o, w_out, x2)


def kernel(x, nsa_w_in, nsa_pe_k, nsa_pe_v, nsa_cmp_w1_k, nsa_cmp_w2_k, nsa_cmp_w1_v, nsa_cmp_w2_v, nsa_w_out, shared_norm, shared_w_kv, shared_b_f, fox_w_q, fox_w_out, attn_norm, ffn_norm, router_w, router_b, moe_w_gate, moe_w_up, moe_w_down, final_norm):
    batch, seq, d = x.shape
    depth = attn_norm.shape[0]
    n_a = nsa_w_in.shape[0]
    x2 = x.reshape(batch * seq, d)
    shared = None
    for layer in range(depth):
        if layer < n_a:
            a = layer
            x2 = _nsa_layer(x2, attn_norm[layer], nsa_w_in[a], nsa_pe_k[a], nsa_pe_v[a],
                            nsa_cmp_w1_k[a], nsa_cmp_w2_k[a], nsa_cmp_w1_v[a], nsa_cmp_w2_v[a],
                            nsa_w_out[a], batch, seq)
        else:
            b = layer - n_a
            x2 = _fox_layer(x2, attn_norm[layer], fox_w_q[b], fox_w_out[b], shared, batch, seq)
        x2 = _moe(x2, ffn_norm[layer], router_w, router_b, moe_w_gate[layer], moe_w_up[layer],
                  moe_w_down[layer])
        if layer == n_a - 1:
            shared = _shared_kv(x2, shared_norm, shared_w_kv, shared_b_f, batch, seq)
    return _rmsnorm(x2, final_norm).reshape(batch, seq, d)
```

```python
import functools

import numpy as np
import jax
import jax.numpy as jnp
from jax import lax
from jax.experimental import pallas as pl
from jax.experimental.pallas import tpu as pltpu

F32 = jnp.float32
BF16 = jnp.bfloat16
I32 = jnp.int32

D_MODEL = 2048
N_HEADS = 16
HEAD_DIM = 128
KV_GROUPS = 4
Q_PER_GROUP = N_HEADS // KV_GROUPS
CMP_BLOCK = 32
CMP_STRIDE = 16
CMP_HIDDEN = 256
SEL_BLOCK = 64
SEL_TOP_N = 16
WINDOW = 512
N_EXPERTS = 32
N_EXPERT_GROUPS = 4
EXPERTS_PER_GROUP = N_EXPERTS // N_EXPERT_GROUPS
TOP_K = 2
D_EXPERT = 1024
RMS_EPS = 1e-6
N_GATES = 3 * N_HEADS

LANES = 128
VMEM_LIMIT = 48 * 1024 * 1024
ROW_TILE = 512
COL_TILE = 512
NORM_CHUNK = 128
Q_TILE = 256
KV_TILE = 256
MOE_ROWS = 256
MOE_F_CHUNK = 512
MASK_VALUE = -1e30

_NT = (((1,), (1,)), ((), ()))


def _params(sem):
    return pltpu.CompilerParams(dimension_semantics=sem, vmem_limit_bytes=VMEM_LIMIT)


def _split3(x):
    a = x.astype(BF16)
    r = x - a.astype(F32)
    b = r.astype(BF16)
    c = (r - b.astype(F32)).astype(BF16)
    return a, b, c


def _split3_by_truncation(x):
    def top(v):
        bits = lax.bitcast_convert_type(v, jnp.uint32) & jnp.uint32(0xFFFF0000)
        return lax.bitcast_convert_type(bits, F32)
    a = top(x)
    b = top(x - a)
    c = x - a - b
    return a.astype(BF16), b.astype(BF16), c.astype(BF16)


def _act(kind, v):
    if kind == "sigmoid":
        return jax.nn.sigmoid(v)
    if kind == "log_sigmoid":
        return jnp.minimum(v, 0.0) - jnp.log1p(jnp.exp(-jnp.abs(v)))
    raise ValueError(kind)


def _norm_proj_kernel(*refs, n_scaled, scale, extra):
    if extra:
        x_ref, g_ref, w_ref, wx_ref, bx_ref, o_ref, ox_ref, h_scr = refs
    else:
        x_ref, g_ref, w_ref, o_ref, h_scr = refs
    j = pl.program_id(1)
    tm = x_ref.shape[0]

    @pl.when(j == 0)
    def _():
        def body(c, carry):
            r0 = pl.multiple_of(c * NORM_CHUNK, NORM_CHUNK)
            x = x_ref[pl.ds(r0, NORM_CHUNK), :]
            ms = jnp.mean(x * x, axis=-1, keepdims=True)
            hb = ((x * lax.rsqrt(ms + RMS_EPS)) * g_ref[...]).astype(BF16)
            h_scr[pl.ds(r0, NORM_CHUNK), :] = hb
            if extra:
                e = jnp.dot(hb, wx_ref[...], preferred_element_type=F32) + bx_ref[...]
                ox_ref[pl.ds(r0, NORM_CHUNK), :] = _act(extra, e)
            return carry
        lax.fori_loop(0, tm // NORM_CHUNK, body, 0)

    acc = jnp.dot(h_scr[...], w_ref[...], preferred_element_type=F32)
    if n_scaled:
        acc = acc * jnp.where(j < n_scaled, jnp.float32(scale), jnp.float32(1.0))
    for c in range(o_ref.shape[0]):
        o_ref[c] = acc[:, c * LANES:(c + 1) * LANES].astype(o_ref.dtype)


def _norm_proj(x2, gain, w, *, n_scaled_cols=0, scale=1.0, wx=None, bx=None, extra=None):
    n, d = x2.shape
    cols = w.shape[1]
    assert n % ROW_TILE == 0 and cols % COL_TILE == 0
    grid = (n // ROW_TILE, cols // COL_TILE)
    in_specs = [
        pl.BlockSpec((ROW_TILE, d), lambda i, j: (i, 0)),
        pl.BlockSpec((1, d), lambda i, j: (0, 0)),
        pl.BlockSpec((d, COL_TILE), lambda i, j: (0, j)),
    ]
    args = [x2, gain.reshape(1, d).astype(F32), w.astype(BF16)]
    out_shape = [jax.ShapeDtypeStruct((cols // LANES, n, LANES), BF16)]
    out_specs = [pl.BlockSpec((COL_TILE // LANES, ROW_TILE, LANES), lambda i, j: (j, i, 0))]
    if extra:
        in_specs += [pl.BlockSpec((d, LANES), lambda i, j: (0, 0)),
                     pl.BlockSpec((1, LANES), lambda i, j: (0, 0))]
        pad = LANES - wx.shape[1]
        args += [jnp.pad(wx, ((0, 0), (0, pad))).astype(BF16),
                 jnp.pad(bx.astype(F32), (0, pad)).reshape(1, LANES)]
        out_shape.append(jax.ShapeDtypeStruct((n, LANES), F32))
        out_specs.append(pl.BlockSpec((ROW_TILE, LANES), lambda i, j: (i, 0)))
    kern = functools.partial(_norm_proj_kernel, n_scaled=n_scaled_cols // COL_TILE,
                             scale=scale, extra=extra)
    res = pl.pallas_call(
        kern, grid=grid, in_specs=in_specs, out_specs=out_specs, out_shape=out_shape,
        scratch_shapes=[pltpu.VMEM((ROW_TILE, d), BF16)],
        compiler_params=_params(("parallel", "arbitrary")), name="norm_proj",
    )(*args)
    return res if extra else res[0]


def _gelu_tanh(v):
    c = np.sqrt(2.0 / np.pi).astype(np.float32)
    return 0.5 * v * (1.0 + jnp.tanh(c * (v + 0.044715 * (v * v * v))))


def _compress_kernel(c_ref, pe_ref, w1_ref, w2_ref, o_ref):
    half = w1_ref.shape[0] // 2
    ch = c_ref[...].astype(F32)
    top = (ch + pe_ref[0:1, :]).astype(BF16)
    bot = (ch + pe_ref[1:2, :]).astype(BF16)
    a = jnp.dot(top, w1_ref[0:half, :], preferred_element_type=F32)
    b = jnp.dot(bot, w1_ref[half:, :], preferred_element_type=F32)
    n = a.shape[0]
    b_next = jnp.concatenate([b[1:, :], jnp.zeros((1, b.shape[1]), F32)], axis=0)
    hid = _gelu_tanh(a + b_next).astype(BF16)
    out = jnp.dot(hid, w2_ref[...], preferred_element_type=F32)
    row = lax.broadcasted_iota(I32, out.shape, 0)
    o_ref[...] = jnp.where(row < n - 1, out, 0.0).astype(o_ref.dtype)


def _compress(proj, head0, pe, w1, w2, batch, seq):
    chunks = seq // CMP_STRIDE
    width = CMP_STRIDE * HEAD_DIM
    view = proj.reshape(proj.shape[0], batch, chunks, width)
    pe2 = pe.astype(F32).reshape(2, width)
    return pl.pallas_call(
        _compress_kernel, grid=(batch, KV_GROUPS),
        in_specs=[
            pl.BlockSpec((None, None, chunks, width), lambda b, g: (head0 + g, b, 0, 0)),
            pl.BlockSpec((2, width), lambda b, g: (0, 0)),
            pl.BlockSpec((2 * width, CMP_HIDDEN), lambda b, g: (0, 0)),
            pl.BlockSpec((CMP_HIDDEN, HEAD_DIM), lambda b, g: (0, 0)),
        ],
        out_specs=pl.BlockSpec((None, None, chunks, HEAD_DIM), lambda b, g: (b, g, 0, 0)),
        out_shape=jax.ShapeDtypeStruct((batch, KV_GROUPS, chunks, HEAD_DIM), BF16),
        compiler_params=_params(("parallel", "parallel")), name="nsa_compress",
    )(view, pe2, w1.astype(BF16), w2.astype(BF16))


def _flash_init(m_scr, acc_scr):
    m_scr[...] = jnp.full(m_scr.shape, MASK_VALUE, F32)
    acc_scr[...] = jnp.zeros(acc_scr.shape, F32)


def _twice(v):
    return jnp.concatenate([v, v], axis=1)


def _flash_step(q_aug, k_aug, v_aug, valid, m_scr, acc_scr):
    s = lax.dot_general(q_aug, k_aug, _NT, preferred_element_type=F32)
    if valid is not None:
        s = jnp.concatenate(
            [jnp.where(valid, s[r * Q_TILE:(r + 1) * Q_TILE], MASK_VALUE)
             for r in range(Q_PER_GROUP)], axis=0)
    m_prev = m_scr[...]
    m_new = jnp.maximum(m_prev, jnp.max(s, axis=-1, keepdims=True))
    alpha = jnp.exp(m_prev - m_new)
    p = jnp.exp(s - _twice(m_new))
    acc_scr[...] = _twice(alpha) * acc_scr[...] + jnp.dot(p.astype(BF16), v_aug,
                                                           preferred_element_type=F32)
    m_scr[...] = m_new


def _flash_out(acc_scr):
    acc = acc_scr[...]
    return acc[:, :HEAD_DIM] / acc[:, HEAD_DIM:]


def _augment_queries(q_ref, qx_ref):
    q = q_ref[...].reshape(Q_PER_GROUP * Q_TILE, HEAD_DIM)
    qx_tab = qx_ref[...]
    qx = jnp.concatenate(
        [jnp.broadcast_to(qx_tab[r:r + 1, :], (Q_TILE, LANES)) for r in range(Q_PER_GROUP)], axis=0)
    return jnp.concatenate([q, qx.astype(BF16)], axis=1)


def _kv_tile(k_ref, kx_ref, v_ref, k0):
    k_aug = jnp.concatenate([k_ref[pl.ds(k0, KV_TILE), :], kx_ref[pl.ds(k0, KV_TILE), :]], axis=1)
    v_aug = jnp.concatenate([v_ref[pl.ds(k0, KV_TILE), :], jnp.ones((KV_TILE, HEAD_DIM), BF16)],
                            axis=1)
    return k_aug, v_aug


def _nsa_attn_kernel(q_ref, qx_ref, kc_ref, kcx_ref, vc_ref, ks_ref, vs_ref, kw_ref, vw_ref,
                     kx_ref, gate_ref, ovl_ref, exp_ref, o_ref, m_scr, acc_scr, *, n_cmp, n_sel):
    i = pl.program_id(2)
    q0 = i * Q_TILE
    q_aug = _augment_queries(q_ref, qx_ref)

    kc_aug = jnp.concatenate([kc_ref[...], kcx_ref[...]], axis=1)
    s = lax.dot_general(q_aug, kc_aug, _NT, preferred_element_type=F32)
    t_loc = lax.broadcasted_iota(I32, (Q_TILE, LANES), 0)
    n_loc = lax.broadcasted_iota(I32, (Q_TILE, LANES), 1)
    block_end = n_loc * CMP_STRIDE + (CMP_BLOCK - 1)
    valid_c = jnp.logical_and(q0 + t_loc >= block_end, n_loc < n_cmp)
    probs = []
    for r in range(Q_PER_GROUP):
        s_r = jnp.where(valid_c, s[r * Q_TILE:(r + 1) * Q_TILE, :], -jnp.inf)
        m = jnp.max(s_r, axis=-1, keepdims=True)
        m = jnp.where(m == -jnp.inf, 0.0, m)
        e = jnp.exp(s_r - m)
        d = jnp.sum(e, axis=-1, keepdims=True)
        probs.append(e / jnp.where(d > 0, d, 1.0))
    p_all = jnp.concatenate(probs, axis=0)
    o_cmp = jnp.dot(p_all.astype(BF16), vc_ref[...], preferred_element_type=F32)

    p_sum_t = (probs[0] + probs[1] + probs[2] + probs[3]).T
    imp = sum(jnp.dot(ovl_ref[...], piece, preferred_element_type=F32)
              for piece in _split3(p_sum_t))
    j_idx = lax.broadcasted_iota(I32, (n_sel, Q_TILE), 0)
    cur = (q0 + lax.broadcasted_iota(I32, (n_sel, Q_TILE), 1)) // SEL_BLOCK
    forced = jnp.logical_or(j_idx == 0, jnp.logical_or(j_idx == cur, j_idx == cur - 1))
    imp = jnp.where(j_idx > cur, -jnp.inf, jnp.where(forced, jnp.inf, imp))
    rank = jnp.zeros((n_sel, Q_TILE), F32)
    for k in range(n_sel):
        row = imp[k:k + 1, :]
        ahead = jnp.logical_or(row > imp, jnp.logical_and(row == imp, j_idx > k))
        rank = rank + jnp.where(ahead, 1.0, 0.0)
    chosen_t = jnp.where(rank < SEL_TOP_N, 1.0, 0.0)
    chosen = jnp.concatenate([chosen_t, jnp.zeros((LANES - n_sel, Q_TILE), F32)], axis=0)
    chosen = chosen.T.astype(BF16)

    q_loc = lax.broadcasted_iota(I32, (Q_TILE, KV_TILE), 0)
    k_loc = lax.broadcasted_iota(I32, (Q_TILE, KV_TILE), 1)
    causal = q_loc >= k_loc
    diag0 = pl.multiple_of(i * KV_TILE, KV_TILE)

    def picked(kb):
        return jnp.dot(chosen, exp_ref[kb], preferred_element_type=F32) > 0.5

    _flash_init(m_scr, acc_scr)

    def sel_body(kb, carry):
        k_aug, v_aug = _kv_tile(ks_ref, kx_ref, vs_ref, pl.multiple_of(kb * KV_TILE, KV_TILE))
        _flash_step(q_aug, k_aug, v_aug, picked(kb), m_scr, acc_scr)
        return carry
    lax.fori_loop(0, i, sel_body, 0)
    k_aug, v_aug = _kv_tile(ks_ref, kx_ref, vs_ref, diag0)
    _flash_step(q_aug, k_aug, v_aug, jnp.logical_and(picked(i), causal), m_scr, acc_scr)
    o_slc = _flash_out(acc_scr)

    _flash_init(m_scr, acc_scr)
    n_back = WINDOW // KV_TILE

    @pl.when(i >= n_back)
    def _():
        k_aug, v_aug = _kv_tile(kw_ref, kx_ref, vw_ref,
                                pl.multiple_of((i - n_back) * KV_TILE, KV_TILE))
        _flash_step(q_aug, k_aug, v_aug, k_loc > q_loc, m_scr, acc_scr)

    def win_body(kb, carry):
        k_aug, v_aug = _kv_tile(kw_ref, kx_ref, vw_ref, pl.multiple_of(kb * KV_TILE, KV_TILE))
        _flash_step(q_aug, k_aug, v_aug, None, m_scr, acc_scr)
        return carry
    lax.fori_loop(jnp.maximum(i - n_back + 1, 0), i, win_body, 0)
    k_aug, v_aug = _kv_tile(kw_ref, kx_ref, vw_ref, diag0)
    _flash_step(q_aug, k_aug, v_aug, causal, m_scr, acc_scr)
    o_win = _flash_out(acc_scr)

    gates = gate_ref[...]
    for r in range(Q_PER_GROUP):
        sl = slice(r * Q_TILE, (r + 1) * Q_TILE)
        g_cmp, g_slc, g_win = (gates[:, 3 * r + b:3 * r + b + 1] for b in range(3))
        o_r = g_cmp * o_cmp[sl] + g_slc * o_slc[sl] + g_win * o_win[sl]
        o_ref[:, r * HEAD_DIM:(r + 1) * HEAD_DIM] = o_r.astype(o_ref.dtype)


def _cmp_to_sel_overlap(n_cmp, n_sel):
    cs = np.arange(LANES)[None, :] * CMP_STRIDE
    ce = cs + CMP_BLOCK
    ss = np.arange(n_sel)[:, None] * SEL_BLOCK
    se = ss + SEL_BLOCK
    ov = np.clip(np.minimum(ce, se) - np.maximum(cs, ss), 0, None) / CMP_BLOCK
    ov = ov * (np.arange(LANES)[None, :] < n_cmp)
    return jnp.asarray(ov, BF16)


def _sel_block_expander(seq):
    n_kv = seq // KV_TILE
    blk = (np.arange(n_kv)[:, None, None] * KV_TILE + np.arange(KV_TILE)[None, None, :]) // SEL_BLOCK
    return jnp.asarray(blk == np.arange(LANES)[None, :, None], BF16)


def _alibi_slopes():
    sl = 2.0 ** (-8.0 * np.arange(1, N_HEADS + 1) / N_HEADS)
    return jnp.asarray(sl, F32)


def _position_bias_dims(pos):
    hi = (pos // 256) * 256
    lo = pos % 256
    cols = np.stack([hi, lo] * 3, axis=1).astype(np.float32)
    return jnp.asarray(np.pad(cols, ((0, 0), (0, LANES - cols.shape[1]))), BF16)


def _slope_bias_dims():
    rest = (2.0 ** (-8.0 * np.arange(1, N_HEADS + 1) / N_HEADS)).astype(np.float32)
    pieces = []
    for _ in range(3):
        piece = rest.astype(BF16).astype(np.float32)
        pieces.append(piece)
        rest = rest - piece
    cols = np.stack([pieces[0], pieces[0], pieces[1], pieces[1], pieces[2], pieces[2]], axis=1)
    tab = cols.reshape(KV_GROUPS, Q_PER_GROUP, 6)
    return jnp.asarray(np.pad(tab, ((0, 0), (0, 8 - Q_PER_GROUP), (0, LANES - 6))))


def _nsa_attention(proj, kcmp, vcmp, gates, batch, seq):
    n = batch * seq
    nq = seq // Q_TILE
    n_cmp = (seq - CMP_BLOCK) // CMP_STRIDE + 1
    n_sel = seq // SEL_BLOCK
    assert seq // CMP_STRIDE == LANES and n_sel <= LANES and Q_TILE == KV_TILE
    assert WINDOW % KV_TILE == 0 and n_sel <= SEL_TOP_N * 2
    kv_heads = [N_HEADS + 2 * KV_GROUPS + k * KV_GROUPS for k in range(4)]
    gates_g = gates[:, :N_GATES].reshape(n, KV_GROUPS, 3 * Q_PER_GROUP).transpose(1, 0, 2)
    gates_g = jnp.pad(gates_g, ((0, 0), (0, 0), (0, LANES - 3 * Q_PER_GROUP)))

    def kv_spec(h0):
        return pl.BlockSpec((None, seq, HEAD_DIM), lambda b, g, i: (h0 + g, b, 0))

    def const_spec(shape):
        return pl.BlockSpec(shape, lambda b, g, i: (0,) * len(shape))

    cmp_spec = pl.BlockSpec((None, None, LANES, HEAD_DIM), lambda b, g, i: (b, g, 0, 0))
    rows = Q_PER_GROUP * Q_TILE
    return pl.pallas_call(
        functools.partial(_nsa_attn_kernel, n_cmp=n_cmp, n_sel=n_sel),
        grid=(batch, KV_GROUPS, nq),
        in_specs=[
            pl.BlockSpec((Q_PER_GROUP, Q_TILE, HEAD_DIM), lambda b, g, i: (g, b * nq + i, 0)),
            pl.BlockSpec((None, 8, LANES), lambda b, g, i: (g, 0, 0)),
            cmp_spec, const_spec((LANES, LANES)), cmp_spec,
            kv_spec(kv_heads[0]), kv_spec(kv_heads[1]), kv_spec(kv_heads[2]), kv_spec(kv_heads[3]),
            const_spec((seq, LANES)),
            pl.BlockSpec((None, Q_TILE, LANES), lambda b, g, i: (g, b * nq + i, 0)),
            const_spec((n_sel, LANES)),
            const_spec((seq // KV_TILE, LANES, KV_TILE)),
        ],
        out_specs=pl.BlockSpec((Q_TILE, Q_PER_GROUP * HEAD_DIM), lambda b, g, i: (b * nq + i, g)),
        out_shape=jax.ShapeDtypeStruct((n, D_MODEL), BF16),
        scratch_shapes=[pltpu.VMEM((rows, LANES), F32), pltpu.VMEM((rows, 2 * HEAD_DIM), F32)],
        compiler_params=_params(("parallel", "parallel", "arbitrary")), name="nsa_attention",
    )(proj, _slope_bias_dims(), kcmp,
      _position_bias_dims(np.arange(LANES) * CMP_STRIDE + (CMP_BLOCK - 1)), vcmp,
      proj, proj, proj, proj, _position_bias_dims(np.arange(seq)), gates_g,
      _cmp_to_sel_overlap(n_cmp, n_sel), _sel_block_expander(seq))


def _fox_attn_kernel(q_ref, qx_ref, k_ref, kx_ref, v_ref, o_ref, m_scr, acc_scr):
    i = pl.program_id(2)
    q_aug = _augment_queries(q_ref, qx_ref)
    _flash_init(m_scr, acc_scr)

    def body(kb, carry):
        k_aug, v_aug = _kv_tile(k_ref, kx_ref, v_ref, pl.multiple_of(kb * KV_TILE, KV_TILE))
        _flash_step(q_aug, k_aug, v_aug, None, m_scr, acc_scr)
        return carry
    lax.fori_loop(0, i, body, 0)
    k_aug, v_aug = _kv_tile(k_ref, kx_ref, v_ref, pl.multiple_of(i * KV_TILE, KV_TILE))
    causal = (lax.broadcasted_iota(I32, (Q_TILE, KV_TILE), 0)
              >= lax.broadcasted_iota(I32, (Q_TILE, KV_TILE), 1))
    _flash_step(q_aug, k_aug, v_aug, causal, m_scr, acc_scr)
    out = _flash_out(acc_scr)
    for r in range(Q_PER_GROUP):
        o_ref[:, r * HEAD_DIM:(r + 1) * HEAD_DIM] = out[r * Q_TILE:(r + 1) * Q_TILE].astype(o_ref.dtype)


def _fox_attention(q_hm, kv_hm, kx, batch, seq):
    n = batch * seq
    nq = seq // Q_TILE
    assert Q_TILE == KV_TILE
    qx = np.zeros((8, LANES), np.float32)
    for r in range(Q_PER_GROUP):
        qx[r, 3 * r:3 * r + 3] = -1.0
    rows = Q_PER_GROUP * Q_TILE
    return pl.pallas_call(
        _fox_attn_kernel, grid=(batch, KV_GROUPS, nq),
        in_specs=[
            pl.BlockSpec((Q_PER_GROUP, Q_TILE, HEAD_DIM), lambda b, g, i: (g, b * nq + i, 0)),
            pl.BlockSpec((8, LANES), lambda b, g, i: (0, 0)),
            pl.BlockSpec((None, seq, HEAD_DIM), lambda b, g, i: (g, b, 0)),
            pl.BlockSpec((None, None, seq, LANES), lambda b, g, i: (b, g, 0, 0)),
            pl.BlockSpec((None, seq, HEAD_DIM), lambda b, g, i: (KV_GROUPS + g, b, 0)),
        ],
        out_specs=pl.BlockSpec((Q_TILE, Q_PER_GROUP * HEAD_DIM), lambda b, g, i: (b * nq + i, g)),
        out_shape=jax.ShapeDtypeStruct((n, D_MODEL), BF16),
        scratch_shapes=[pltpu.VMEM((rows, LANES), F32), pltpu.VMEM((rows, 2 * HEAD_DIM), F32)],
        compiler_params=_params(("parallel", "parallel", "arbitrary")), name="fox_attention",
    )(q_hm, jnp.asarray(qx), kv_hm, kx, kv_hm)


def _cumsum_kernel(x_ref, tri_ref, o_ref, carry_scr):
    @pl.when(pl.program_id(1) == 0)
    def _():
        carry_scr[...] = jnp.zeros(carry_scr.shape, F32)
    tri = tri_ref[...]
    c = sum(jnp.dot(tri, piece, preferred_element_type=F32) for piece in _split3(x_ref[...]))
    c = c + carry_scr[...]
    o_ref[...] = c
    carry_scr[...] = c[ROW_TILE - 1:ROW_TILE, :]


def _cumsum_seq(x2, batch, seq):
    nt = seq // ROW_TILE
    tri = jnp.asarray(np.tril(np.ones((ROW_TILE, ROW_TILE))), BF16)
    return pl.pallas_call(
        _cumsum_kernel, grid=(batch, nt),
        in_specs=[pl.BlockSpec((ROW_TILE, LANES), lambda b, t: (b * nt + t, 0)),
                  pl.BlockSpec((ROW_TILE, ROW_TILE), lambda b, t: (0, 0))],
        out_specs=pl.BlockSpec((ROW_TILE, LANES), lambda b, t: (b * nt + t, 0)),
        out_shape=jax.ShapeDtypeStruct(x2.shape, F32),
        scratch_shapes=[pltpu.VMEM((1, LANES), F32)],
        compiler_params=_params(("parallel", "arbitrary")), name="cumsum_seq",
    )(x2, tri)


def _proj_residual_kernel(o_ref, w_ref, x_ref, out_ref):
    out_ref[...] = x_ref[...] + jnp.dot(o_ref[...], w_ref[...], preferred_element_type=F32)


def _proj_residual(o2, w, x2):
    n, d = x2.shape
    return pl.pallas_call(
        _proj_residual_kernel, grid=(n // ROW_TILE, d // COL_TILE),
        in_specs=[pl.BlockSpec((ROW_TILE, o2.shape[1]), lambda i, j: (i, 0)),
                  pl.BlockSpec((o2.shape[1], COL_TILE), lambda i, j: (0, j)),
                  pl.BlockSpec((ROW_TILE, COL_TILE), lambda i, j: (i, j))],
        out_specs=pl.BlockSpec((ROW_TILE, COL_TILE), lambda i, j: (i, j)),
        out_shape=jax.ShapeDtypeStruct((n, d), F32),
        compiler_params=_params(("parallel", "parallel")), name="proj_residual",
    )(o2, w.astype(BF16), x2)


def _top2_rows(v):
    idx = lax.broadcasted_iota(I32, v.shape, 0).astype(F32)
    m1 = jnp.max(v, axis=0, keepdims=True)
    i1 = jnp.min(jnp.where(v == m1, idx, EXPERTS_PER_GROUP), axis=0, keepdims=True)
    v2 = jnp.where(idx == i1, -jnp.inf, v)
    m2 = jnp.max(v2, axis=0, keepdims=True)
    i2 = jnp.min(jnp.where(v2 == m2, idx, EXPERTS_PER_GROUP), axis=0, keepdims=True)
    return m1, i1, m2, i2


def _router_kernel(x_ref, g_ref, wr_ref, br_ref, h_ref, e_ref, p_ref):
    x = x_ref[...]
    ms = jnp.mean(x * x, axis=-1, keepdims=True)
    h = (x * lax.rsqrt(ms + RMS_EPS)) * g_ref[...]
    h_ref[...] = h.astype(h_ref.dtype)
    h1, h2, _ = _split3(h)
    w1, w2, _ = _split3(wr_ref[...])
    logits = None
    for a, b in ((w1, h1), (w1, h2), (w2, h1)):
        part = lax.dot_general(a, b, _NT, preferred_element_type=F32)
        logits = part if logits is None else logits + part
    m = jnp.max(logits, axis=0, keepdims=True)
    e = jnp.exp(logits - m)
    probs = e / jnp.sum(e, axis=0, keepdims=True)
    sel = probs + br_ref[...]
    idx = lax.broadcasted_iota(I32, (EXPERTS_PER_GROUP, x.shape[0]), 0).astype(F32)
    best_score = best_e1 = best_e2 = best_p1 = best_p2 = None
    for k in range(N_EXPERT_GROUPS):
        rows = slice(k * EXPERTS_PER_GROUP, (k + 1) * EXPERTS_PER_GROUP)
        m1, i1, m2, i2 = _top2_rows(sel[rows])
        pk = probs[rows]
        p1 = jnp.sum(jnp.where(idx == i1, pk, 0.0), axis=0, keepdims=True)
        p2 = jnp.sum(jnp.where(idx == i2, pk, 0.0), axis=0, keepdims=True)
        score = m1 + m2
        e1 = i1 + k * EXPERTS_PER_GROUP
        e2 = i2 + k * EXPERTS_PER_GROUP
        if k == 0:
            best_score, best_e1, best_e2, best_p1, best_p2 = score, e1, e2, p1, p2
        else:
            take = score > best_score
            best_score = jnp.where(take, score, best_score)
            best_e1 = jnp.where(take, e1, best_e1)
            best_e2 = jnp.where(take, e2, best_e2)
            best_p1 = jnp.where(take, p1, best_p1)
            best_p2 = jnp.where(take, p2, best_p2)
    tot = best_p1 + best_p2
    e_ref[...] = jnp.concatenate([best_e1, best_e2], axis=0).astype(I32)
    p_ref[...] = jnp.concatenate([best_p1 / tot, best_p2 / tot], axis=0)


def _router(x2, gain, w_router, b_router):
    n, d = x2.shape
    tm = ROW_TILE
    return pl.pallas_call(
        _router_kernel, grid=(n // tm,),
        in_specs=[pl.BlockSpec((tm, d), lambda i: (i, 0)),
                  pl.BlockSpec((1, d), lambda i: (0, 0)),
                  pl.BlockSpec((N_EXPERTS, d), lambda i: (0, 0)),
                  pl.BlockSpec((N_EXPERTS, 1), lambda i: (0, 0))],
        out_specs=[pl.BlockSpec((tm, d), lambda i: (i, 0)),
                   pl.BlockSpec((TOP_K, tm), lambda i: (0, i)),
                   pl.BlockSpec((TOP_K, tm), lambda i: (0, i))],
        out_shape=[jax.ShapeDtypeStruct((n, d), BF16),
                   jax.ShapeDtypeStruct((TOP_K, n), I32),
                   jax.ShapeDtypeStruct((TOP_K, n), F32)],
        compiler_params=_params(("parallel",)), name="router",
    )(x2, gain.reshape(1, d).astype(F32), w_router.T.astype(F32),
      b_router.astype(F32).reshape(N_EXPERTS, 1))


def _expert_kernel(be_ref, nu_ref, x_ref, wg_ref, wu_ref, wd_ref, o_ref):
    i = pl.program_id(0)

    @pl.when(i < nu_ref[0])
    def _():
        x = x_ref[...]
        for c in range(D_EXPERT // MOE_F_CHUNK):
            cols = slice(c * MOE_F_CHUNK, (c + 1) * MOE_F_CHUNK)
            gt = jnp.dot(x, wg_ref[:, cols], preferred_element_type=F32)
            up = jnp.dot(x, wu_ref[:, cols], preferred_element_type=F32)
            act = (gt * jax.nn.sigmoid(gt) * up).astype(BF16)
            y = jnp.dot(act, wd_ref[cols, :], preferred_element_type=F32)
            if c == 0:
                o_ref[...] = y
            else:
                o_ref[...] += y

    @pl.when(i >= nu_ref[0])
    def _():
        o_ref[...] = jnp.zeros(o_ref.shape, o_ref.dtype)


def _expert_mlp(xb, block_e, n_used, w_gate, w_up, w_down):
    rows, d = xb.shape
    n_blocks = rows // MOE_ROWS
    f = w_gate.shape[2]
    grid_spec = pltpu.PrefetchScalarGridSpec(
        num_scalar_prefetch=2, grid=(n_blocks,),
        in_specs=[pl.BlockSpec((MOE_ROWS, d), lambda i, be, nu: (i, 0)),
                  pl.BlockSpec((None, d, f), lambda i, be, nu: (be[i], 0, 0)),
                  pl.BlockSpec((None, d, f), lambda i, be, nu: (be[i], 0, 0)),
                  pl.BlockSpec((None, f, d), lambda i, be, nu: (be[i], 0, 0))],
        out_specs=pl.BlockSpec((MOE_ROWS, d), lambda i, be, nu: (i, 0)),
    )
    return pl.pallas_call(
        _expert_kernel, grid_spec=grid_spec,
        out_shape=jax.ShapeDtypeStruct((rows, d), F32),
        compiler_params=_params(("arbitrary",)), name="expert_mlp",
    )(block_e, n_used, xb, w_gate, w_up, w_down)


def _moe(x2, gain, w_router, b_router, w_gate, w_up, w_down):
    n, d = x2.shape
    h, e_idx, e_w = _router(x2, gain, w_router, b_router)
    flat_e = e_idx.T.reshape(-1)
    n_asg = n * TOP_K
    onehot = (flat_e[:, None] == jnp.arange(N_EXPERTS, dtype=I32)[None, :]).astype(I32)
    counts = jnp.sum(onehot, axis=0)
    rank = jnp.sum((jnp.cumsum(onehot, axis=0) - onehot) * onehot, axis=1)
    blocks_per_e = (counts + MOE_ROWS - 1) // MOE_ROWS
    blk_end = jnp.cumsum(blocks_per_e)
    row_start = (blk_end - blocks_per_e) * MOE_ROWS
    dest = row_start[flat_e] + rank
    n_blocks = -(-(n_asg + N_EXPERTS * (MOE_ROWS - 1)) // MOE_ROWS)
    slot_tok = jnp.zeros((n_blocks * MOE_ROWS,), I32).at[dest].set(
        jnp.arange(n_asg, dtype=I32) // TOP_K)
    n_used = blk_end[-1:].astype(I32)
    blk = jnp.minimum(jnp.arange(n_blocks, dtype=I32), n_used[0] - 1)
    block_e = jnp.sum((blk_end[None, :] <= blk[:, None]).astype(I32), axis=1)
    block_e = jnp.minimum(block_e, N_EXPERTS - 1)
    xb = h[slot_tok]
    yb = _expert_mlp(xb, block_e, n_used, w_gate.astype(BF16), w_up.astype(BF16),
                     w_down.astype(BF16))
    y = yb[dest].reshape(n, TOP_K, d) * e_w.T[:, :, None]
    return x2 + jnp.sum(y, axis=1)


def _rmsnorm_kernel(x_ref, g_ref, o_ref):
    x = x_ref[...]
    ms = jnp.mean(x * x, axis=-1, keepdims=True)
    o_ref[...] = (x * lax.rsqrt(ms + RMS_EPS)) * g_ref[...]


def _rmsnorm(x2, gain):
    n, d = x2.shape
    tm = NORM_CHUNK * 2
    return pl.pallas_call(
        _rmsnorm_kernel, grid=(n // tm,),
        in_specs=[pl.BlockSpec((tm, d), lambda i: (i, 0)), pl.BlockSpec((1, d), lambda i: (0, 0))],
        out_specs=pl.BlockSpec((tm, d), lambda i: (i, 0)),
        out_shape=jax.ShapeDtypeStruct((n, d), F32),
        compiler_params=_params(("parallel",)), name="final_norm",
    )(x2, gain.reshape(1, d).astype(F32))


def _nsa_layer(x2, gain, w_in, pe_k, pe_v, w1_k, w2_k, w1_v, w2_v, w_out, batch, seq):
    main_cols = D_MODEL + 6 * KV_GROUPS * HEAD_DIM
    proj, gates = _norm_proj(
        x2, gain, w_in[:, :main_cols], n_scaled_cols=D_MODEL, scale=HEAD_DIM ** -0.5,
        wx=w_in[:, main_cols:], bx=jnp.zeros((N_GATES,), F32), extra="sigmoid")
    kcmp = _compress(proj, N_HEADS, pe_k, w1_k, w2_k, batch, seq)
    vcmp = _compress(proj, N_HEADS + KV_GROUPS, pe_v, w1_v, w2_v, batch, seq)
    o = _nsa_attention(proj, kcmp, vcmp, gates, batch, seq)
    return _proj_residual(o, w_out, x2)


def _shared_kv(x2, gain, w_kv, b_f, batch, seq):
    kv_cols = 2 * KV_GROUPS * HEAD_DIM
    kv_hm, logf = _norm_proj(x2, gain, w_kv[:, :kv_cols], wx=w_kv[:, kv_cols:], bx=b_f,
                             extra="log_sigmoid")
    c_tok = _cumsum_seq(logf, batch, seq)
    pieces = jnp.stack(_split3_by_truncation(c_tok[:, :N_HEADS]), axis=-1)
    kx = pieces.reshape(batch, seq, KV_GROUPS, 3 * Q_PER_GROUP).transpose(0, 2, 1, 3)
    kx = jnp.pad(kx, ((0, 0), (0, 0), (0, 0), (0, LANES - 3 * Q_PER_GROUP)))
    return kv_hm, kx


def _fox_layer(x2, gain, w_q, w_out, shared, batch, seq):
    kv_hm, kx = shared
    q_hm = _norm_proj(x2, gain, w_q, n_scaled_cols=D_MODEL, scale=HEAD_DIM ** -0.5)
    o = _fox_attention(q_hm, kv_hm, kx, batch, seq)
    return _proj_residual(o, w_out, x2)


def kernel(x, nsa_w_in, nsa_pe_k, nsa_pe_v, nsa_cmp_w1_k, nsa_cmp_w2_k, nsa_cmp_w1_v, nsa_cmp_w2_v, nsa_w_out, shared_norm, shared_w_kv, shared_b_f, fox_w_q, fox_w_out, attn_norm, ffn_norm, router_w, router_b, moe_w_gate, moe_w_up, moe_w_down, final_norm):
    batch, seq, d = x.shape
    depth = attn_norm.shape[0]
    n_a = nsa_w_in.shape[0]
    x2 = x.reshape(batch * seq, d)
    shared = None
    for layer in range(depth):
        if layer < n_a:
            a = layer
            x2 = _nsa_layer(x2, attn_norm[layer], nsa_w_in[a], nsa_pe_k[a], nsa_pe_v[a],
                            nsa_cmp_w1_k[a], nsa_cmp_w2_k[a], nsa_cmp_w1_v[a], nsa_cmp_w2_v[a],
                            nsa_w_out[a], batch, seq)
        else:
            b = layer - n_a
            x2 = _fox_layer(x2, attn_norm[layer], fox_w_q[b], fox_w_out[b], shared, batch, seq)
        x2 = _moe(x2, ffn_norm[layer], router_w, router_b, moe_w_gate[layer], moe_w_up[layer],
                  moe_w_down[layer])
        if layer == n_a - 1:
            shared = _shared_kv(x2, shared_norm, shared_w_kv, shared_b_f, batch, seq)
    return _rmsnorm(x2, final_norm).reshape(batch, seq, d)
```

```python
import functools

import numpy as np
import jax
import jax.numpy as jnp
from jax import lax
from jax.experimental import pallas as pl
from jax.experimental.pallas import tpu as pltpu

F32 = jnp.float32
BF16 = jnp.bfloat16
I32 = jnp.int32

D_MODEL = 2048
N_HEADS = 16
HEAD_DIM = 128
KV_GROUPS = 4
Q_PER_GROUP = N_HEADS // KV_GROUPS
CMP_BLOCK = 32
CMP_STRIDE = 16
CMP_HIDDEN = 256
SEL_BLOCK = 64
SEL_TOP_N = 16
WINDOW = 512
N_EXPERTS = 32
N_EXPERT_GROUPS = 4
EXPERTS_PER_GROUP = N_EXPERTS // N_EXPERT_GROUPS
TOP_K = 2
D_EXPERT = 1024
RMS_EPS = 1e-6
N_GATES = 3 * N_HEADS

LANES = 128
VMEM_LIMIT = 48 * 1024 * 1024
ROW_TILE = 512
COL_TILE = 512
NORM_CHUNK = 128
Q_TILE = 256
KV_TILE = 256
MOE_ROWS = 256
MOE_F_CHUNK = 512
MASK_VALUE = -1e30

_NT = (((1,), (1,)), ((), ()))


def _params(sem):
    return pltpu.CompilerParams(dimension_semantics=sem, vmem_limit_bytes=VMEM_LIMIT)


def _split3(x):
    a = x.astype(BF16)
    r = x - a.astype(F32)
    b = r.astype(BF16)
    c = (r - b.astype(F32)).astype(BF16)
    return a, b, c


def _split3_by_truncation(x):
    def top(v):
        bits = lax.bitcast_convert_type(v, jnp.uint32) & jnp.uint32(0xFFFF0000)
        return lax.bitcast_convert_type(bits, F32)
    a = top(x)
    b = top(x - a)
    c = x - a - b
    return a.astype(BF16), b.astype(BF16), c.astype(BF16)


def _act(kind, v):
    if kind == "sigmoid":
        return jax.nn.sigmoid(v)
    if kind == "log_sigmoid":
        return jnp.minimum(v, 0.0) - jnp.log1p(jnp.exp(-jnp.abs(v)))
    raise ValueError(kind)


def _norm_proj_kernel(*refs, n_scaled, scale, extra):
    if extra:
        x_ref, g_ref, w_ref, wx_ref, bx_ref, o_ref, ox_ref, h_scr = refs
    else:
        x_ref, g_ref, w_ref, o_ref, h_scr = refs
    j = pl.program_id(1)
    tm = x_ref.shape[0]

    @pl.when(j == 0)
    def _():
        def body(c, carry):
            r0 = pl.multiple_of(c * NORM_CHUNK, NORM_CHUNK)
            x = x_ref[pl.ds(r0, NORM_CHUNK), :]
            ms = jnp.mean(x * x, axis=-1, keepdims=True)
            hb = ((x * lax.rsqrt(ms + RMS_EPS)) * g_ref[...]).astype(BF16)
            h_scr[pl.ds(r0, NORM_CHUNK), :] = hb
            if extra:
                e = jnp.dot(hb, wx_ref[...], preferred_element_type=F32) + bx_ref[...]
                ox_ref[pl.ds(r0, NORM_CHUNK), :] = _act(extra, e)
            return carry
        lax.fori_loop(0, tm // NORM_CHUNK, body, 0)

    acc = jnp.dot(h_scr[...], w_ref[...], preferred_element_type=F32)
    if n_scaled:
        acc = acc * jnp.where(j < n_scaled, jnp.float32(scale), jnp.float32(1.0))
    for c in range(o_ref.shape[0]):
        o_ref[c] = acc[:, c * LANES:(c + 1) * LANES].astype(o_ref.dtype)


def _norm_proj(x2, gain, w, *, n_scaled_cols=0, scale=1.0, wx=None, bx=None, extra=None):
    n, d = x2.shape
    cols = w.shape[1]
    assert n % ROW_TILE == 0 and cols % COL_TILE == 0
    grid = (n // ROW_TILE, cols // COL_TILE)
    in_specs = [
        pl.BlockSpec((ROW_TILE, d), lambda i, j: (i, 0)),
        pl.BlockSpec((1, d), lambda i, j: (0, 0)),
        pl.BlockSpec((d, COL_TILE), lambda i, j: (0, j)),
    ]
    args = [x2, gain.reshape(1, d).astype(F32), w.astype(BF16)]
    out_shape = [jax.ShapeDtypeStruct((cols // LANES, n, LANES), BF16)]
    out_specs = [pl.BlockSpec((COL_TILE // LANES, ROW_TILE, LANES), lambda i, j: (j, i, 0))]
    if extra:
        in_specs += [pl.BlockSpec((d, LANES), lambda i, j: (0, 0)),
                     pl.BlockSpec((1, LANES), lambda i, j: (0, 0))]
        pad = LANES - wx.shape[1]
        args += [jnp.pad(wx, ((0, 0), (0, pad))).astype(BF16),
                 jnp.pad(bx.astype(F32), (0, pad)).reshape(1, LANES)]
        out_shape.append(jax.ShapeDtypeStruct((n, LANES), F32))
        out_specs.append(pl.BlockSpec((ROW_TILE, LANES), lambda i, j: (i, 0)))
    kern = functools.partial(_norm_proj_kernel, n_scaled=n_scaled_cols // COL_TILE,
                             scale=scale, extra=extra)
    res = pl.pallas_call(
        kern, grid=grid, in_specs=in_specs, out_specs=out_specs, out_shape=out_shape,
        scratch_shapes=[pltpu.VMEM((ROW_TILE, d), BF16)],
        compiler_params=_params(("parallel", "arbitrary")), name="norm_proj",
    )(*args)
    return res if extra else res[0]


def _gelu_tanh(v):
    c = np.sqrt(2.0 / np.pi).astype(np.float32)
    return 0.5 * v * (1.0 + jnp.tanh(c * (v + 0.044715 * (v * v * v))))


def _compress_kernel(c_ref, pe_ref, w1_ref, w2_ref, o_ref):
    half = w1_ref.shape[0] // 2
    ch = c_ref[...].astype(F32)
    top = (ch + pe_ref[0:1, :]).astype(BF16)
    bot = (ch + pe_ref[1:2, :]).astype(BF16)
    a = jnp.dot(top, w1_ref[0:half, :], preferred_element_type=F32)
    b = jnp.dot(bot, w1_ref[half:, :], preferred_element_type=F32)
    n = a.shape[0]
    b_next = jnp.concatenate([b[1:, :], jnp.zeros((1, b.shape[1]), F32)], axis=0)
    hid = _gelu_tanh(a + b_next).astype(BF16)
    out = jnp.dot(hid, w2_ref[...], preferred_element_type=F32)
    row = lax.broadcasted_iota(I32, out.shape, 0)
    o_ref[...] = jnp.where(row < n - 1, out, 0.0).astype(o_ref.dtype)


def _compress(proj, head0, pe, w1, w2, batch, seq):
    chunks = seq // CMP_STRIDE
    width = CMP_STRIDE * HEAD_DIM
    view = proj[head0:head0 + KV_GROUPS].reshape(KV_GROUPS, batch, chunks, width)
    pe2 = pe.astype(F32).reshape(2, width)
    return pl.pallas_call(
        _compress_kernel, grid=(batch, KV_GROUPS),
        in_specs=[
            pl.BlockSpec((None, None, chunks, width), lambda b, g: (g, b, 0, 0)),
            pl.BlockSpec((2, width), lambda b, g: (0, 0)),
            pl.BlockSpec((2 * width, CMP_HIDDEN), lambda b, g: (0, 0)),
            pl.BlockSpec((CMP_HIDDEN, HEAD_DIM), lambda b, g: (0, 0)),
        ],
        out_specs=pl.BlockSpec((None, None, chunks, HEAD_DIM), lambda b, g: (b, g, 0, 0)),
        out_shape=jax.ShapeDtypeStruct((batch, KV_GROUPS, chunks, HEAD_DIM), BF16),
        compiler_params=_params(("parallel", "parallel")), name="nsa_compress",
    )(view, pe2, w1.astype(BF16), w2.astype(BF16))


def _flash_init(m_scr, acc_scr):
    m_scr[...] = jnp.full(m_scr.shape, MASK_VALUE, F32)
    acc_scr[...] = jnp.zeros(acc_scr.shape, F32)


def _twice(v):
    return jnp.concatenate([v, v], axis=1)


def _flash_step(q_aug, k_aug, v_aug, valid, m_scr, acc_scr):
    s = lax.dot_general(q_aug, k_aug, _NT, preferred_element_type=F32)
    if valid is not None:
        s = jnp.concatenate(
            [jnp.where(valid, s[r * Q_TILE:(r + 1) * Q_TILE], MASK_VALUE)
             for r in range(Q_PER_GROUP)], axis=0)
    m_prev = m_scr[...]
    m_new = jnp.maximum(m_prev, jnp.max(s, axis=-1, keepdims=True))
    alpha = jnp.exp(m_prev - m_new)
    p = jnp.exp(s - _twice(m_new))
    acc_scr[...] = _twice(alpha) * acc_scr[...] + jnp.dot(p.astype(BF16), v_aug,
                                                           preferred_element_type=F32)
    m_scr[...] = m_new


def _flash_out(acc_scr):
    acc = acc_scr[...]
    return acc[:, :HEAD_DIM] / acc[:, HEAD_DIM:]


def _augment_queries(q_ref, qx_ref):
    q = q_ref[...].reshape(Q_PER_GROUP * Q_TILE, HEAD_DIM)
    qx_tab = qx_ref[...]
    qx = jnp.concatenate(
        [jnp.broadcast_to(qx_tab[r:r + 1, :], (Q_TILE, LANES)) for r in range(Q_PER_GROUP)], axis=0)
    return jnp.concatenate([q, qx.astype(BF16)], axis=1)


def _kv_tile(k_ref, kx_ref, v_ref, k0):
    k_aug = jnp.concatenate([k_ref[pl.ds(k0, KV_TILE), :], kx_ref[pl.ds(k0, KV_TILE), :]], axis=1)
    v_aug = jnp.concatenate([v_ref[pl.ds(k0, KV_TILE), :], jnp.ones((KV_TILE, HEAD_DIM), BF16)],
                            axis=1)
    return k_aug, v_aug


def _nsa_attn_kernel(q_ref, qx_ref, kc_ref, kcx_ref, vc_ref, ks_ref, vs_ref, kw_ref, vw_ref,
                     kx_ref, gate_ref, ovl_ref, exp_ref, o_ref, m_scr, acc_scr, *, n_cmp, n_sel):
    i = pl.program_id(2)
    q0 = i * Q_TILE
    q_aug = _augment_queries(q_ref, qx_ref)

    kc_aug = jnp.concatenate([kc_ref[...], kcx_ref[...]], axis=1)
    s = lax.dot_general(q_aug, kc_aug, _NT, preferred_element_type=F32)
    t_loc = lax.broadcasted_iota(I32, (Q_TILE, LANES), 0)
    n_loc = lax.broadcasted_iota(I32, (Q_TILE, LANES), 1)
    block_end = n_loc * CMP_STRIDE + (CMP_BLOCK - 1)
    valid_c = jnp.logical_and(q0 + t_loc >= block_end, n_loc < n_cmp)
    probs = []
    for r in range(Q_PER_GROUP):
        s_r = jnp.where(valid_c, s[r * Q_TILE:(r + 1) * Q_TILE, :], -jnp.inf)
        m = jnp.max(s_r, axis=-1, keepdims=True)
        m = jnp.where(m == -jnp.inf, 0.0, m)
        e = jnp.exp(s_r - m)
        d = jnp.sum(e, axis=-1, keepdims=True)
        probs.append(e / jnp.where(d > 0, d, 1.0))
    p_all = jnp.concatenate(probs, axis=0)
    o_cmp = jnp.dot(p_all.astype(BF16), vc_ref[...], preferred_element_type=F32)

    p_sum_t = (probs[0] + probs[1] + probs[2] + probs[3]).T
    imp = sum(jnp.dot(ovl_ref[...], piece, preferred_element_type=F32)
              for piece in _split3(p_sum_t))
    j_idx = lax.broadcasted_iota(I32, (n_sel, Q_TILE), 0)
    cur = (q0 + lax.broadcasted_iota(I32, (n_sel, Q_TILE), 1)) // SEL_BLOCK
    forced = jnp.logical_or(j_idx == 0, jnp.logical_or(j_idx == cur, j_idx == cur - 1))
    imp = jnp.where(j_idx > cur, -jnp.inf, jnp.where(forced, jnp.inf, imp))
    rank = jnp.zeros((n_sel, Q_TILE), F32)
    for k in range(n_sel):
        row = imp[k:k + 1, :]
        ahead = jnp.logical_or(row > imp, jnp.logical_and(row == imp, j_idx > k))
        rank = rank + jnp.where(ahead, 1.0, 0.0)
    chosen_t = jnp.where(rank < SEL_TOP_N, 1.0, 0.0)
    chosen = jnp.concatenate([chosen_t, jnp.zeros((LANES - n_sel, Q_TILE), F32)], axis=0)
    chosen = chosen.T.astype(BF16)

    q_loc = lax.broadcasted_iota(I32, (Q_TILE, KV_TILE), 0)
    k_loc = lax.broadcasted_iota(I32, (Q_TILE, KV_TILE), 1)
    causal = q_loc >= k_loc
    diag0 = pl.multiple_of(i * KV_TILE, KV_TILE)

    def picked(kb):
        return jnp.dot(chosen, exp_ref[kb], preferred_element_type=F32) > 0.5

    _flash_init(m_scr, acc_scr)

    def sel_body(kb, carry):
        k_aug, v_aug = _kv_tile(ks_ref, kx_ref, vs_ref, pl.multiple_of(kb * KV_TILE, KV_TILE))
        _flash_step(q_aug, k_aug, v_aug, picked(kb), m_scr, acc_scr)
        return carry
    lax.fori_loop(0, i, sel_body, 0)
    k_aug, v_aug = _kv_tile(ks_ref, kx_ref, vs_ref, diag0)
    _flash_step(q_aug, k_aug, v_aug, jnp.logical_and(picked(i), causal), m_scr, acc_scr)
    o_slc = _flash_out(acc_scr)

    _flash_init(m_scr, acc_scr)
    n_back = WINDOW // KV_TILE

    @pl.when(i >= n_back)
    def _():
        k_aug, v_aug = _kv_tile(kw_ref, kx_ref, vw_ref,
                                pl.multiple_of((i - n_back) * KV_TILE, KV_TILE))
        _flash_step(q_aug, k_aug, v_aug, k_loc > q_loc, m_scr, acc_scr)

    def win_body(kb, carry):
        k_aug, v_aug = _kv_tile(kw_ref, kx_ref, vw_ref, pl.multiple_of(kb * KV_TILE, KV_TILE))
        _flash_step(q_aug, k_aug, v_aug, None, m_scr, acc_scr)
        return carry
    lax.fori_loop(jnp.maximum(i - n_back + 1, 0), i, win_body, 0)
    k_aug, v_aug = _kv_tile(kw_ref, kx_ref, vw_ref, diag0)
    _flash_step(q_aug, k_aug, v_aug, causal, m_scr, acc_scr)
    o_win = _flash_out(acc_scr)

    gates = gate_ref[...]
    for r in range(Q_PER_GROUP):
        sl = slice(r * Q_TILE, (r + 1) * Q_TILE)
        g_cmp, g_slc, g_win = (gates[:, 3 * r + b:3 * r + b + 1] for b in range(3))
        o_r = g_cmp * o_cmp[sl] + g_slc * o_slc[sl] + g_win * o_win[sl]
        o_ref[:, r * HEAD_DIM:(r + 1) * HEAD_DIM] = o_r.astype(o_ref.dtype)


def _cmp_to_sel_overlap(n_cmp, n_sel):
    cs = np.arange(LANES)[None, :] * CMP_STRIDE
    ce = cs + CMP_BLOCK
    ss = np.arange(n_sel)[:, None] * SEL_BLOCK
    se = ss + SEL_BLOCK
    ov = np.clip(np.minimum(ce, se) - np.maximum(cs, ss), 0, None) / CMP_BLOCK
    ov = ov * (np.arange(LANES)[None, :] < n_cmp)
    return jnp.asarray(ov, BF16)


def _sel_block_expander(seq):
    n_kv = seq // KV_TILE
    blk = (np.arange(n_kv)[:, None, None] * KV_TILE + np.arange(KV_TILE)[None, None, :]) // SEL_BLOCK
    return jnp.asarray(blk == np.arange(LANES)[None, :, None], BF16)


def _alibi_slopes():
    sl = 2.0 ** (-8.0 * np.arange(1, N_HEADS + 1) / N_HEADS)
    return jnp.asarray(sl, F32)


def _position_bias_dims(pos):
    hi = (pos // 256) * 256
    lo = pos % 256
    cols = np.stack([hi, lo] * 3, axis=1).astype(np.float32)
    return jnp.asarray(np.pad(cols, ((0, 0), (0, LANES - cols.shape[1]))), BF16)


def _slope_bias_dims():
    rest = (2.0 ** (-8.0 * np.arange(1, N_HEADS + 1) / N_HEADS)).astype(np.float32)
    pieces = []
    for _ in range(3):
        piece = rest.astype(BF16).astype(np.float32)
        pieces.append(piece)
        rest = rest - piece
    cols = np.stack([pieces[0], pieces[0], pieces[1], pieces[1], pieces[2], pieces[2]], axis=1)
    tab = cols.reshape(KV_GROUPS, Q_PER_GROUP, 6)
    return jnp.asarray(np.pad(tab, ((0, 0), (0, 8 - Q_PER_GROUP), (0, LANES - 6))))


def _nsa_attention(proj, kcmp, vcmp, gates, batch, seq):
    n = batch * seq
    nq = seq // Q_TILE
    n_cmp = (seq - CMP_BLOCK) // CMP_STRIDE + 1
    n_sel = seq // SEL_BLOCK
    assert seq // CMP_STRIDE == LANES and n_sel <= LANES and Q_TILE == KV_TILE
    assert WINDOW % KV_TILE == 0 and n_sel <= SEL_TOP_N * 2
    kv_heads = [N_HEADS + 2 * KV_GROUPS + k * KV_GROUPS for k in range(4)]
    gates_g = gates[:, :N_GATES].reshape(n, KV_GROUPS, 3 * Q_PER_GROUP).transpose(1, 0, 2)
    gates_g = jnp.pad(gates_g, ((0, 0), (0, 0), (0, LANES - 3 * Q_PER_GROUP)))

    def kv_spec(h0):
        return pl.BlockSpec((None, seq, HEAD_DIM), lambda b, g, i: (h0 + g, b, 0))

    def const_spec(shape):
        return pl.BlockSpec(shape, lambda b, g, i: (0,) * len(shape))

    cmp_spec = pl.BlockSpec((None, None, LANES, HEAD_DIM), lambda b, g, i: (b, g, 0, 0))
    rows = Q_PER_GROUP * Q_TILE
    return pl.pallas_call(
        functools.partial(_nsa_attn_kernel, n_cmp=n_cmp, n_sel=n_sel),
        grid=(batch, KV_GROUPS, nq),
        in_specs=[
            pl.BlockSpec((Q_PER_GROUP, Q_TILE, HEAD_DIM), lambda b, g, i: (g, b * nq + i, 0)),
            pl.BlockSpec((None, 8, LANES), lambda b, g, i: (g, 0, 0)),
            cmp_spec, const_spec((LANES, LANES)), cmp_spec,
            kv_spec(kv_heads[0]), kv_spec(kv_heads[1]), kv_spec(kv_heads[2]), kv_spec(kv_heads[3]),
            const_spec((seq, LANES)),
            pl.BlockSpec((None, Q_TILE, LANES), lambda b, g, i: (g, b * nq + i, 0)),
            const_spec((n_sel, LANES)),
            const_spec((seq // KV_TILE, LANES, KV_TILE)),
        ],
        out_specs=pl.BlockSpec((Q_TILE, Q_PER_GROUP * HEAD_DIM), lambda b, g, i: (b * nq + i, g)),
        out_shape=jax.ShapeDtypeStruct((n, D_MODEL), BF16),
        scratch_shapes=[pltpu.VMEM((rows, LANES), F32), pltpu.VMEM((rows, 2 * HEAD_DIM), F32)],
        compiler_params=_params(("parallel", "parallel", "arbitrary")), name="nsa_attention",
    )(proj, _slope_bias_dims(), kcmp,
      _position_bias_dims(np.arange(LANES) * CMP_STRIDE + (CMP_BLOCK - 1)), vcmp,
      proj, proj, proj, proj, _position_bias_dims(np.arange(seq)), gates_g,
      _cmp_to_sel_overlap(n_cmp, n_sel), _sel_block_expander(seq))


def _fox_attn_kernel(q_ref, qx_ref, k_ref, kx_ref, v_ref, o_ref, m_scr, acc_scr):
    i = pl.program_id(2)
    q_aug = _augment_queries(q_ref, qx_ref)
    _flash_init(m_scr, acc_scr)

    def body(kb, carry):
        k_aug, v_aug = _kv_tile(k_ref, kx_ref, v_ref, pl.multiple_of(kb * KV_TILE, KV_TILE))
        _flash_step(q_aug, k_aug, v_aug, None, m_scr, acc_scr)
        return carry
    lax.fori_loop(0, i, body, 0)
    k_aug, v_aug = _kv_tile(k_ref, kx_ref, v_ref, pl.multiple_of(i * KV_TILE, KV_TILE))
    causal = (lax.broadcasted_iota(I32, (Q_TILE, KV_TILE), 0)
              >= lax.broadcasted_iota(I32, (Q_TILE, KV_TILE), 1))
    _flash_step(q_aug, k_aug, v_aug, causal, m_scr, acc_scr)
    out = _flash_out(acc_scr)
    for r in range(Q_PER_GROUP):
        o_ref[:, r * HEAD_DIM:(r + 1) * HEAD_DIM] = out[r * Q_TILE:(r + 1) * Q_TILE].astype(o_ref.dtype)


def _fox_attention(q_hm, kv_hm, kx, batch, seq):
    n = batch * seq
    nq = seq // Q_TILE
    assert Q_TILE == KV_TILE
    qx = np.zeros((8, LANES), np.float32)
    for r in range(Q_PER_GROUP):
        qx[r, 3 * r:3 * r + 3] = -1.0
    rows = Q_PER_GROUP * Q_TILE
    return pl.pallas_call(
        _fox_attn_kernel, grid=(batch, KV_GROUPS, nq),
        in_specs=[
            pl.BlockSpec((Q_PER_GROUP, Q_TILE, HEAD_DIM), lambda b, g, i: (g, b * nq + i, 0)),
            pl.BlockSpec((8, LANES), lambda b, g, i: (0, 0)),
            pl.BlockSpec((None, seq, HEAD_DIM), lambda b, g, i: (g, b, 0)),
            pl.BlockSpec((None, None, seq, LANES), lambda b, g, i: (b, g, 0, 0)),
            pl.BlockSpec((None, seq, HEAD_DIM), lambda b, g, i: (KV_GROUPS + g, b, 0)),
        ],
        out_specs=pl.BlockSpec((Q_TILE, Q_PER_GROUP * HEAD_DIM), lambda b, g, i: (b * nq + i, g)),
        out_shape=jax.ShapeDtypeStruct((n, D_MODEL), BF16),
        scratch_shapes=[pltpu.VMEM((rows, LANES), F32), pltpu.VMEM((rows, 2 * HEAD_DIM), F32)],
        compiler_params=_params(("parallel", "parallel", "arbitrary")), name="fox_attention",
    )(q_hm, jnp.asarray(qx), kv_hm, kx, kv_hm)


def _cumsum_kernel(x_ref, tri_ref, o_ref, carry_scr):
    @pl.when(pl.program_id(1) == 0)
    def _():
        carry_scr[...] = jnp.zeros(carry_scr.shape, F32)
    tri = tri_ref[...]
    c = sum(jnp.dot(tri, piece, preferred_element_type=F32) for piece in _split3(x_ref[...]))
    c = c + carry_scr[...]
    o_ref[...] = c
    carry_scr[...] = c[ROW_TILE - 1:ROW_TILE, :]


def _cumsum_seq(x2, batch, seq):
    nt = seq // ROW_TILE
    tri = jnp.asarray(np.tril(np.ones((ROW_TILE, ROW_TILE))), BF16)
    return pl.pallas_call(
        _cumsum_kernel, grid=(batch, nt),
        in_specs=[pl.BlockSpec((ROW_TILE, LANES), lambda b, t: (b * nt + t, 0)),
                  pl.BlockSpec((ROW_TILE, ROW_TILE), lambda b, t: (0, 0))],
        out_specs=pl.BlockSpec((ROW_TILE, LANES), lambda b, t: (b * nt + t, 0)),
        out_shape=jax.ShapeDtypeStruct(x2.shape, F32),
        scratch_shapes=[pltpu.VMEM((1, LANES), F32)],
        compiler_params=_params(("parallel", "arbitrary")), name="cumsum_seq",
    )(x2, tri)


def _proj_residual_kernel(o_ref, w_ref, x_ref, out_ref):
    out_ref[...] = x_ref[...] + jnp.dot(o_ref[...], w_ref[...], preferred_element_type=F32)


def _proj_residual(o2, w, x2):
    n, d = x2.shape
    return pl.pallas_call(
        _proj_residual_kernel, grid=(n // ROW_TILE, d // COL_TILE),
        in_specs=[pl.BlockSpec((ROW_TILE, o2.shape[1]), lambda i, j: (i, 0)),
                  pl.BlockSpec((o2.shape[1], COL_TILE), lambda i, j: (0, j)),
                  pl.BlockSpec((ROW_TILE, COL_TILE), lambda i, j: (i, j))],
        out_specs=pl.BlockSpec((ROW_TILE, COL_TILE), lambda i, j: (i, j)),
        out_shape=jax.ShapeDtypeStruct((n, d), F32),
        compiler_params=_params(("parallel", "parallel")), name="proj_residual",
    )(o2, w.astype(BF16), x2)


def _top2_rows(v):
    idx = lax.broadcasted_iota(I32, v.shape, 0).astype(F32)
    m1 = jnp.max(v, axis=0, keepdims=True)
    i1 = jnp.min(jnp.where(v == m1, idx, EXPERTS_PER_GROUP), axis=0, keepdims=True)
    v2 = jnp.where(idx == i1, -jnp.inf, v)
    m2 = jnp.max(v2, axis=0, keepdims=True)
    i2 = jnp.min(jnp.where(v2 == m2, idx, EXPERTS_PER_GROUP), axis=0, keepdims=True)
    return m1, i1, m2, i2


def _router_kernel(x_ref, g_ref, wr_ref, br_ref, h_ref, e_ref, p_ref):
    x = x_ref[...]
    ms = jnp.mean(x * x, axis=-1, keepdims=True)
    h = (x * lax.rsqrt(ms + RMS_EPS)) * g_ref[...]
    h_ref[...] = h.astype(h_ref.dtype)
    h1, h2, _ = _split3(h)
    w1, w2, _ = _split3(wr_ref[...])
    logits = None
    for a, b in ((w1, h1), (w1, h2), (w2, h1)):
        part = lax.dot_general(a, b, _NT, preferred_element_type=F32)
        logits = part if logits is None else logits + part
    m = jnp.max(logits, axis=0, keepdims=True)
    e = jnp.exp(logits - m)
    probs = e / jnp.sum(e, axis=0, keepdims=True)
    sel = probs + br_ref[...]
    idx = lax.broadcasted_iota(I32, (EXPERTS_PER_GROUP, x.shape[0]), 0).astype(F32)
    best_score = best_e1 = best_e2 = best_p1 = best_p2 = None
    for k in range(N_EXPERT_GROUPS):
        rows = slice(k * EXPERTS_PER_GROUP, (k + 1) * EXPERTS_PER_GROUP)
        m1, i1, m2, i2 = _top2_rows(sel[rows])
        pk = probs[rows]
        p1 = jnp.sum(jnp.where(idx == i1, pk, 0.0), axis=0, keepdims=True)
        p2 = jnp.sum(jnp.where(idx == i2, pk, 0.0), axis=0, keepdims=True)
        score = m1 + m2
        e1 = i1 + k * EXPERTS_PER_GROUP
        e2 = i2 + k * EXPERTS_PER_GROUP
        if k == 0:
            best_score, best_e1, best_e2, best_p1, best_p2 = score, e1, e2, p1, p2
        else:
            take = score > best_score
            best_score = jnp.where(take, score, best_score)
            best_e1 = jnp.where(take, e1, best_e1)
            best_e2 = jnp.where(take, e2, best_e2)
            best_p1 = jnp.where(take, p1, best_p1)
            best_p2 = jnp.where(take, p2, best_p2)
    tot = best_p1 + best_p2
    e_ref[...] = jnp.concatenate([best_e1, best_e2], axis=0).astype(I32)
    p_ref[...] = jnp.concatenate([best_p1 / tot, best_p2 / tot], axis=0)


def _router(x2, gain, w_router, b_router):
    n, d = x2.shape
    tm = ROW_TILE
    return pl.pallas_call(
        _router_kernel, grid=(n // tm,),
        in_specs=[pl.BlockSpec((tm, d), lambda i: (i, 0)),
                  pl.BlockSpec((1, d), lambda i: (0, 0)),
                  pl.BlockSpec((N_EXPERTS, d), lambda i: (0, 0)),
                  pl.BlockSpec((N_EXPERTS, 1), lambda i: (0, 0))],
        out_specs=[pl.BlockSpec((tm, d), lambda i: (i, 0)),
                   pl.BlockSpec((TOP_K, tm), lambda i: (0, i)),
                   pl.BlockSpec((TOP_K, tm), lambda i: (0, i))],
        out_shape=[jax.ShapeDtypeStruct((n, d), BF16),
                   jax.ShapeDtypeStruct((TOP_K, n), I32),
                   jax.ShapeDtypeStruct((TOP_K, n), F32)],
        compiler_params=_params(("parallel",)), name="router",
    )(x2, gain.reshape(1, d).astype(F32), w_router.T.astype(F32),
      b_router.astype(F32).reshape(N_EXPERTS, 1))


def _expert_kernel(be_ref, nu_ref, x_ref, wg_ref, wu_ref, wd_ref, o_ref):
    i = pl.program_id(0)

    @pl.when(i < nu_ref[0])
    def _():
        x = x_ref[...]
        for c in range(D_EXPERT // MOE_F_CHUNK):
            cols = slice(c * MOE_F_CHUNK, (c + 1) * MOE_F_CHUNK)
            gt = jnp.dot(x, wg_ref[:, cols], preferred_element_type=F32)
            up = jnp.dot(x, wu_ref[:, cols], preferred_element_type=F32)
            act = (gt * jax.nn.sigmoid(gt) * up).astype(BF16)
            y = jnp.dot(act, wd_ref[cols, :], preferred_element_type=F32)
            if c == 0:
                o_ref[...] = y
            else:
                o_ref[...] += y

    @pl.when(i >= nu_ref[0])
    def _():
        o_ref[...] = jnp.zeros(o_ref.shape, o_ref.dtype)


def _expert_mlp(xb, block_e, n_used, w_gate, w_up, w_down, layer):
    rows, d = xb.shape
    n_blocks = rows // MOE_ROWS
    f = w_gate.shape[3]
    grid_spec = pltpu.PrefetchScalarGridSpec(
        num_scalar_prefetch=2, grid=(n_blocks,),
        in_specs=[pl.BlockSpec((MOE_ROWS, d), lambda i, be, nu: (i, 0)),
                  pl.BlockSpec((None, None, d, f), lambda i, be, nu: (layer, be[i], 0, 0)),
                  pl.BlockSpec((None, None, d, f), lambda i, be, nu: (layer, be[i], 0, 0)),
                  pl.BlockSpec((None, None, f, d), lambda i, be, nu: (layer, be[i], 0, 0))],
        out_specs=pl.BlockSpec((MOE_ROWS, d), lambda i, be, nu: (i, 0)),
    )
    return pl.pallas_call(
        _expert_kernel, grid_spec=grid_spec,
        out_shape=jax.ShapeDtypeStruct((rows, d), F32),
        compiler_params=_params(("arbitrary",)), name="expert_mlp",
    )(block_e, n_used, xb, w_gate, w_up, w_down)


def _moe(x2, gain, w_router, b_router, w_gate, w_up, w_down, layer):
    n, d = x2.shape
    h, e_idx, e_w = _router(x2, gain, w_router, b_router)
    flat_e = e_idx.reshape(-1)
    n_asg = n * TOP_K
    onehot = (flat_e[:, None] == jnp.arange(N_EXPERTS, dtype=I32)[None, :]).astype(I32)
    counts = jnp.sum(onehot, axis=0)
    rank = jnp.sum((jnp.cumsum(onehot, axis=0) - onehot) * onehot, axis=1)
    blocks_per_e = (counts + MOE_ROWS - 1) // MOE_ROWS
    blk_end = jnp.cumsum(blocks_per_e)
    row_start = (blk_end - blocks_per_e) * MOE_ROWS
    dest = row_start[flat_e] + rank
    n_blocks = -(-(n_asg + N_EXPERTS * (MOE_ROWS - 1)) // MOE_ROWS)
    slot_tok = jnp.zeros((n_blocks * MOE_ROWS,), I32).at[dest].set(
        jnp.arange(n_asg, dtype=I32) % n)
    n_used = blk_end[-1:].astype(I32)
    blk = jnp.minimum(jnp.arange(n_blocks, dtype=I32), n_used[0] - 1)
    block_e = jnp.sum((blk_end[None, :] <= blk[:, None]).astype(I32), axis=1)
    block_e = jnp.minimum(block_e, N_EXPERTS - 1)
    xb = h[slot_tok]
    yb = _expert_mlp(xb, block_e, n_used, w_gate, w_up, w_down, layer)
    out = x2
    for k in range(TOP_K):
        out = out + yb[dest[k * n:(k + 1) * n]] * e_w[k][:, None]
    return out


def _rmsnorm_kernel(x_ref, g_ref, o_ref):
    x = x_ref[...]
    ms = jnp.mean(x * x, axis=-1, keepdims=True)
    o_ref[...] = (x * lax.rsqrt(ms + RMS_EPS)) * g_ref[...]


def _rmsnorm(x2, gain):
    n, d = x2.shape
    tm = NORM_CHUNK * 2
    return pl.pallas_call(
        _rmsnorm_kernel, grid=(n // tm,),
        in_specs=[pl.BlockSpec((tm, d), lambda i: (i, 0)), pl.BlockSpec((1, d), lambda i: (0, 0))],
        out_specs=pl.BlockSpec((tm, d), lambda i: (i, 0)),
        out_shape=jax.ShapeDtypeStruct((n, d), F32),
        compiler_params=_params(("parallel",)), name="final_norm",
    )(x2, gain.reshape(1, d).astype(F32))


def _nsa_layer(x2, gain, w_in, pe_k, pe_v, w1_k, w2_k, w1_v, w2_v, w_out, batch, seq):
    main_cols = D_MODEL + 6 * KV_GROUPS * HEAD_DIM
    proj, gates = _norm_proj(
        x2, gain, w_in[:, :main_cols], n_scaled_cols=D_MODEL, scale=HEAD_DIM ** -0.5,
        wx=w_in[:, main_cols:], bx=jnp.zeros((N_GATES,), F32), extra="sigmoid")
    kcmp = _compress(proj, N_HEADS, pe_k, w1_k, w2_k, batch, seq)
    vcmp = _compress(proj, N_HEADS + KV_GROUPS, pe_v, w1_v, w2_v, batch, seq)
    o = _nsa_attention(proj, kcmp, vcmp, gates, batch, seq)
    return _proj_residual(o, w_out, x2)


def _shared_kv(x2, gain, w_kv, b_f, batch, seq):
    kv_cols = 2 * KV_GROUPS * HEAD_DIM
    kv_hm, logf = _norm_proj(x2, gain, w_kv[:, :kv_cols], wx=w_kv[:, kv_cols:], bx=b_f,
                             extra="log_sigmoid")
    c_tok = _cumsum_seq(logf, batch, seq)
    pieces = jnp.stack(_split3_by_truncation(c_tok[:, :N_HEADS]), axis=-1)
    kx = pieces.reshape(batch, seq, KV_GROUPS, 3 * Q_PER_GROUP).transpose(0, 2, 1, 3)
    kx = jnp.pad(kx, ((0, 0), (0, 0), (0, 0), (0, LANES - 3 * Q_PER_GROUP)))
    return kv_hm, kx


def _fox_layer(x2, gain, w_q, w_out, shared, batch, seq):
    kv_hm, kx = shared
    q_hm = _norm_proj(x2, gain, w_q, n_scaled_cols=D_MODEL, scale=HEAD_DIM ** -0.5)
    o = _fox_attention(q_hm, kv_hm, kx, batch, seq)
    return _proj_residual(o, w_out, x2)


def kernel(x, nsa_w_in, nsa_pe_k, nsa_pe_v, nsa_cmp_w1_k, nsa_cmp_w2_k, nsa_cmp_w1_v, nsa_cmp_w2_v, nsa_w_out, shared_norm, shared_w_kv, shared_b_f, fox_w_q, fox_w_out, attn_norm, ffn_norm, router_w, router_b, moe_w_gate, moe_w_up, moe_w_down, final_norm):
    batch, seq, d = x.shape
    depth = attn_norm.shape[0]
    n_a = nsa_w_in.shape[0]
    x2 = x.reshape(batch * seq, d)
    w_gate, w_up, w_down = (w.astype(BF16) for w in (moe_w_gate, moe_w_up, moe_w_down))
    shared = None
    for layer in range(depth):
        if layer < n_a:
            a = layer
            x2 = _nsa_layer(x2, attn_norm[layer], nsa_w_in[a], nsa_pe_k[a], nsa_pe_v[a],
                            nsa_cmp_w1_k[a], nsa_cmp_w2_k[a], nsa_cmp_w1_v[a], nsa_cmp_w2_v[a],
                            nsa_w_out[a], batch, seq)
        else:
            b = layer - n_a
            x2 = _fox_layer(x2, attn_norm[layer], fox_w_q[b], fox_w_out[b], shared, batch, seq)
        x2 = _moe(x2, ffn_norm[layer], router_w, router_b, w_gate, w_up, w_down, layer)
        if layer == n_a - 1:
            shared = _shared_kv(x2, shared_norm, shared_w_kv, shared_b_f, batch, seq)
    return _rmsnorm(x2, final_norm).reshape(batch, seq, d)
```

```python
import functools

import numpy as np
import jax
import jax.numpy as jnp
from jax import lax
from jax.experimental import pallas as pl
from jax.experimental.pallas import tpu as pltpu

F32 = jnp.float32
BF16 = jnp.bfloat16
I32 = jnp.int32

D_MODEL = 2048
N_HEADS = 16
HEAD_DIM = 128
KV_GROUPS = 4
Q_PER_GROUP = N_HEADS // KV_GROUPS
CMP_BLOCK = 32
CMP_STRIDE = 16
CMP_HIDDEN = 256
SEL_BLOCK = 64
SEL_TOP_N = 16
WINDOW = 512
N_EXPERTS = 32
N_EXPERT_GROUPS = 4
EXPERTS_PER_GROUP = N_EXPERTS // N_EXPERT_GROUPS
TOP_K = 2
D_EXPERT = 1024
RMS_EPS = 1e-6
N_GATES = 3 * N_HEADS

LANES = 128
VMEM_LIMIT = 48 * 1024 * 1024
ROW_TILE = 1024
COL_TILE = 1024
SEQ_TILE = 512
NORM_CHUNK = 128
Q_TILE = 256
KV_TILE = 256
MOE_ROWS = 256
MOE_F_CHUNK = 512
MASK_VALUE = -1e30

_NT = (((1,), (1,)), ((), ()))


def _params(sem):
    return pltpu.CompilerParams(dimension_semantics=sem, vmem_limit_bytes=VMEM_LIMIT)


def _split3(x):
    a = x.astype(BF16)
    r = x - a.astype(F32)
    b = r.astype(BF16)
    c = (r - b.astype(F32)).astype(BF16)
    return a, b, c


def _split3_by_truncation(x):
    def top(v):
        bits = lax.bitcast_convert_type(v, jnp.uint32) & jnp.uint32(0xFFFF0000)
        return lax.bitcast_convert_type(bits, F32)
    a = top(x)
    b = top(x - a)
    c = x - a - b
    return a.astype(BF16), b.astype(BF16), c.astype(BF16)


def _act(kind, v):
    if kind == "sigmoid":
        return jax.nn.sigmoid(v)
    if kind == "log_sigmoid":
        return jnp.minimum(v, 0.0) - jnp.log1p(jnp.exp(-jnp.abs(v)))
    raise ValueError(kind)


def _norm_proj_kernel(*refs, n_scaled, scale, extra):
    if extra:
        x_ref, g_ref, w_ref, wx_ref, bx_ref, o_ref, ox_ref, h_scr = refs
    else:
        x_ref, g_ref, w_ref, o_ref, h_scr = refs
    j = pl.program_id(1)
    tm = x_ref.shape[0]

    @pl.when(j == 0)
    def _():
        def body(c, carry):
            r0 = pl.multiple_of(c * NORM_CHUNK, NORM_CHUNK)
            x = x_ref[pl.ds(r0, NORM_CHUNK), :]
            ms = jnp.mean(x * x, axis=-1, keepdims=True)
            hb = ((x * lax.rsqrt(ms + RMS_EPS)) * g_ref[...]).astype(BF16)
            h_scr[pl.ds(r0, NORM_CHUNK), :] = hb
            if extra:
                e = jnp.dot(hb, wx_ref[...], preferred_element_type=F32) + bx_ref[...]
                ox_ref[pl.ds(r0, NORM_CHUNK), :] = _act(extra, e)
            return carry
        lax.fori_loop(0, tm // NORM_CHUNK, body, 0)

    acc = jnp.dot(h_scr[...], w_ref[...], preferred_element_type=F32)
    if n_scaled:
        acc = acc * jnp.where(j < n_scaled, jnp.float32(scale), jnp.float32(1.0))
    for c in range(o_ref.shape[0]):
        o_ref[c] = acc[:, c * LANES:(c + 1) * LANES].astype(o_ref.dtype)


def _norm_proj(x2, gain, w, *, n_scaled_cols=0, scale=1.0, wx=None, bx=None, extra=None):
    n, d = x2.shape
    cols = w.shape[1]
    assert n % ROW_TILE == 0 and cols % COL_TILE == 0
    grid = (n // ROW_TILE, cols // COL_TILE)
    in_specs = [
        pl.BlockSpec((ROW_TILE, d), lambda i, j: (i, 0)),
        pl.BlockSpec((1, d), lambda i, j: (0, 0)),
        pl.BlockSpec((d, COL_TILE), lambda i, j: (0, j)),
    ]
    args = [x2, gain.reshape(1, d).astype(F32), w.astype(BF16)]
    out_shape = [jax.ShapeDtypeStruct((cols // LANES, n, LANES), BF16)]
    out_specs = [pl.BlockSpec((COL_TILE // LANES, ROW_TILE, LANES), lambda i, j: (j, i, 0))]
    if extra:
        in_specs += [pl.BlockSpec((d, LANES), lambda i, j: (0, 0)),
                     pl.BlockSpec((1, LANES), lambda i, j: (0, 0))]
        pad = LANES - wx.shape[1]
        args += [jnp.pad(wx, ((0, 0), (0, pad))).astype(BF16),
                 jnp.pad(bx.astype(F32), (0, pad)).reshape(1, LANES)]
        out_shape.append(jax.ShapeDtypeStruct((n, LANES), F32))
        out_specs.append(pl.BlockSpec((ROW_TILE, LANES), lambda i, j: (i, 0)))
    kern = functools.partial(_norm_proj_kernel, n_scaled=n_scaled_cols // COL_TILE,
                             scale=scale, extra=extra)
    res = pl.pallas_call(
        kern, grid=grid, in_specs=in_specs, out_specs=out_specs, out_shape=out_shape,
        scratch_shapes=[pltpu.VMEM((ROW_TILE, d), BF16)],
        compiler_params=_params(("parallel", "arbitrary")), name="norm_proj",
    )(*args)
    return res if extra else res[0]


def _gelu_tanh(v):
    c = np.sqrt(2.0 / np.pi).astype(np.float32)
    return 0.5 * v * (1.0 + jnp.tanh(c * (v + 0.044715 * (v * v * v))))


def _compress_kernel(c_ref, pe_ref, w1_ref, w2_ref, o_ref):
    half = w1_ref.shape[0] // 2
    ch = c_ref[...].astype(F32)
    top = (ch + pe_ref[0:1, :]).astype(BF16)
    bot = (ch + pe_ref[1:2, :]).astype(BF16)
    a = jnp.dot(top, w1_ref[0:half, :], preferred_element_type=F32)
    b = jnp.dot(bot, w1_ref[half:, :], preferred_element_type=F32)
    n = a.shape[0]
    b_next = jnp.concatenate([b[1:, :], jnp.zeros((1, b.shape[1]), F32)], axis=0)
    hid = _gelu_tanh(a + b_next).astype(BF16)
    out = jnp.dot(hid, w2_ref[...], preferred_element_type=F32)
    row = lax.broadcasted_iota(I32, out.shape, 0)
    o_ref[...] = jnp.where(row < n - 1, out, 0.0).astype(o_ref.dtype)


def _compress(proj, head0, pe, w1, w2, batch, seq):
    chunks = seq // CMP_STRIDE
    width = CMP_STRIDE * HEAD_DIM
    view = proj[head0:head0 + KV_GROUPS].reshape(KV_GROUPS, batch, chunks, width)
    pe2 = pe.astype(F32).reshape(2, width)
    return pl.pallas_call(
        _compress_kernel, grid=(batch, KV_GROUPS),
        in_specs=[
            pl.BlockSpec((None, None, chunks, width), lambda b, g: (g, b, 0, 0)),
            pl.BlockSpec((2, width), lambda b, g: (0, 0)),
            pl.BlockSpec((2 * width, CMP_HIDDEN), lambda b, g: (0, 0)),
            pl.BlockSpec((CMP_HIDDEN, HEAD_DIM), lambda b, g: (0, 0)),
        ],
        out_specs=pl.BlockSpec((None, None, chunks, HEAD_DIM), lambda b, g: (b, g, 0, 0)),
        out_shape=jax.ShapeDtypeStruct((batch, KV_GROUPS, chunks, HEAD_DIM), BF16),
        compiler_params=_params(("parallel", "parallel")), name="nsa_compress",
    )(view, pe2, w1.astype(BF16), w2.astype(BF16))


def _flash_init(m_scr, acc_scr):
    m_scr[...] = jnp.full(m_scr.shape, MASK_VALUE, F32)
    acc_scr[...] = jnp.zeros(acc_scr.shape, F32)


def _twice(v):
    return jnp.concatenate([v, v], axis=1)


def _flash_step(q_aug, k_aug, v_aug, valid, m_scr, acc_scr):
    s = lax.dot_general(q_aug, k_aug, _NT, preferred_element_type=F32)
    if valid is not None:
        s = jnp.concatenate(
            [jnp.where(valid, s[r * Q_TILE:(r + 1) * Q_TILE], MASK_VALUE)
             for r in range(Q_PER_GROUP)], axis=0)
    m_prev = m_scr[...]
    m_new = jnp.maximum(m_prev, jnp.max(s, axis=-1, keepdims=True))
    alpha = jnp.exp(m_prev - m_new)
    p = jnp.exp(s - _twice(m_new))
    acc_scr[...] = _twice(alpha) * acc_scr[...] + jnp.dot(p.astype(BF16), v_aug,
                                                           preferred_element_type=F32)
    m_scr[...] = m_new


def _flash_out(acc_scr):
    acc = acc_scr[...]
    return acc[:, :HEAD_DIM] / acc[:, HEAD_DIM:]


def _augment_queries(q_ref, qx_ref):
    q = q_ref[...].reshape(Q_PER_GROUP * Q_TILE, HEAD_DIM)
    qx_tab = qx_ref[...]
    qx = jnp.concatenate(
        [jnp.broadcast_to(qx_tab[r:r + 1, :], (Q_TILE, LANES)) for r in range(Q_PER_GROUP)], axis=0)
    return jnp.concatenate([q, qx.astype(BF16)], axis=1)


def _kv_tile(k_ref, kx_ref, v_ref, k0):
    k_aug = jnp.concatenate([k_ref[pl.ds(k0, KV_TILE), :], kx_ref[pl.ds(k0, KV_TILE), :]], axis=1)
    v_aug = jnp.concatenate([v_ref[pl.ds(k0, KV_TILE), :], jnp.ones((KV_TILE, HEAD_DIM), BF16)],
                            axis=1)
    return k_aug, v_aug


def _nsa_attn_kernel(q_ref, qx_ref, kc_ref, kcx_ref, vc_ref, ks_ref, vs_ref, kw_ref, vw_ref,
                     kx_ref, gate_ref, ovl_ref, exp_ref, o_ref, m_scr, acc_scr, *, n_cmp, n_sel):
    i = pl.program_id(2)
    q0 = i * Q_TILE
    q_aug = _augment_queries(q_ref, qx_ref)

    kc_aug = jnp.concatenate([kc_ref[...], kcx_ref[...]], axis=1)
    s = lax.dot_general(q_aug, kc_aug, _NT, preferred_element_type=F32)
    t_loc = lax.broadcasted_iota(I32, (Q_TILE, LANES), 0)
    n_loc = lax.broadcasted_iota(I32, (Q_TILE, LANES), 1)
    block_end = n_loc * CMP_STRIDE + (CMP_BLOCK - 1)
    valid_c = jnp.logical_and(q0 + t_loc >= block_end, n_loc < n_cmp)
    probs = []
    for r in range(Q_PER_GROUP):
        s_r = jnp.where(valid_c, s[r * Q_TILE:(r + 1) * Q_TILE, :], -jnp.inf)
        m = jnp.max(s_r, axis=-1, keepdims=True)
        m = jnp.where(m == -jnp.inf, 0.0, m)
        e = jnp.exp(s_r - m)
        d = jnp.sum(e, axis=-1, keepdims=True)
        probs.append(e / jnp.where(d > 0, d, 1.0))
    p_all = jnp.concatenate(probs, axis=0)
    o_cmp = jnp.dot(p_all.astype(BF16), vc_ref[...], preferred_element_type=F32)

    p_sum_t = (probs[0] + probs[1] + probs[2] + probs[3]).T
    imp = sum(jnp.dot(ovl_ref[...], piece, preferred_element_type=F32)
              for piece in _split3(p_sum_t))
    j_idx = lax.broadcasted_iota(I32, (n_sel, Q_TILE), 0)
    cur = (q0 + lax.broadcasted_iota(I32, (n_sel, Q_TILE), 1)) // SEL_BLOCK
    forced = jnp.logical_or(j_idx == 0, jnp.logical_or(j_idx == cur, j_idx == cur - 1))
    imp = jnp.where(j_idx > cur, -jnp.inf, jnp.where(forced, jnp.inf, imp))
    rank = jnp.zeros((n_sel, Q_TILE), F32)
    for k in range(n_sel):
        row = imp[k:k + 1, :]
        ahead = jnp.logical_or(row > imp, jnp.logical_and(row == imp, j_idx > k))
        rank = rank + jnp.where(ahead, 1.0, 0.0)
    chosen_t = jnp.where(rank < SEL_TOP_N, 1.0, 0.0)
    chosen = jnp.concatenate([chosen_t, jnp.zeros((LANES - n_sel, Q_TILE), F32)], axis=0)
    chosen = chosen.T.astype(BF16)

    q_loc = lax.broadcasted_iota(I32, (Q_TILE, KV_TILE), 0)
    k_loc = lax.broadcasted_iota(I32, (Q_TILE, KV_TILE), 1)
    causal = q_loc >= k_loc
    diag0 = pl.multiple_of(i * KV_TILE, KV_TILE)

    def picked(kb):
        return jnp.dot(chosen, exp_ref[kb], preferred_element_type=F32) > 0.5

    _flash_init(m_scr, acc_scr)

    def sel_body(kb, carry):
        k_aug, v_aug = _kv_tile(ks_ref, kx_ref, vs_ref, pl.multiple_of(kb * KV_TILE, KV_TILE))
        _flash_step(q_aug, k_aug, v_aug, picked(kb), m_scr, acc_scr)
        return carry
    lax.fori_loop(0, i, sel_body, 0)
    k_aug, v_aug = _kv_tile(ks_ref, kx_ref, vs_ref, diag0)
    _flash_step(q_aug, k_aug, v_aug, jnp.logical_and(picked(i), causal), m_scr, acc_scr)
    o_slc = _flash_out(acc_scr)

    _flash_init(m_scr, acc_scr)
    n_back = WINDOW // KV_TILE

    @pl.when(i >= n_back)
    def _():
        k_aug, v_aug = _kv_tile(kw_ref, kx_ref, vw_ref,
                                pl.multiple_of((i - n_back) * KV_TILE, KV_TILE))
        _flash_step(q_aug, k_aug, v_aug, k_loc > q_loc, m_scr, acc_scr)

    def win_body(kb, carry):
        k_aug, v_aug = _kv_tile(kw_ref, kx_ref, vw_ref, pl.multiple_of(kb * KV_TILE, KV_TILE))
        _flash_step(q_aug, k_aug, v_aug, None, m_scr, acc_scr)
        return carry
    lax.fori_loop(jnp.maximum(i - n_back + 1, 0), i, win_body, 0)
    k_aug, v_aug = _kv_tile(kw_ref, kx_ref, vw_ref, diag0)
    _flash_step(q_aug, k_aug, v_aug, causal, m_scr, acc_scr)
    o_win = _flash_out(acc_scr)

    gates = gate_ref[...]
    for r in range(Q_PER_GROUP):
        sl = slice(r * Q_TILE, (r + 1) * Q_TILE)
        g_cmp, g_slc, g_win = (gates[:, 3 * r + b:3 * r + b + 1] for b in range(3))
        o_r = g_cmp * o_cmp[sl] + g_slc * o_slc[sl] + g_win * o_win[sl]
        o_ref[:, r * HEAD_DIM:(r + 1) * HEAD_DIM] = o_r.astype(o_ref.dtype)


def _cmp_to_sel_overlap(n_cmp, n_sel):
    cs = np.arange(LANES)[None, :] * CMP_STRIDE
    ce = cs + CMP_BLOCK
    ss = np.arange(n_sel)[:, None] * SEL_BLOCK
    se = ss + SEL_BLOCK
    ov = np.clip(np.minimum(ce, se) - np.maximum(cs, ss), 0, None) / CMP_BLOCK
    ov = ov * (np.arange(LANES)[None, :] < n_cmp)
    return jnp.asarray(ov, BF16)


def _sel_block_expander(seq):
    n_kv = seq // KV_TILE
    blk = (np.arange(n_kv)[:, None, None] * KV_TILE + np.arange(KV_TILE)[None, None, :]) // SEL_BLOCK
    return jnp.asarray(blk == np.arange(LANES)[None, :, None], BF16)


def _alibi_slopes():
    sl = 2.0 ** (-8.0 * np.arange(1, N_HEADS + 1) / N_HEADS)
    return jnp.asarray(sl, F32)


def _position_bias_dims(pos):
    hi = (pos // 256) * 256
    lo = pos % 256
    cols = np.stack([hi, lo] * 3, axis=1).astype(np.float32)
    return jnp.asarray(np.pad(cols, ((0, 0), (0, LANES - cols.shape[1]))), BF16)


def _slope_bias_dims():
    rest = (2.0 ** (-8.0 * np.arange(1, N_HEADS + 1) / N_HEADS)).astype(np.float32)
    pieces = []
    for _ in range(3):
        piece = rest.astype(BF16).astype(np.float32)
        pieces.append(piece)
        rest = rest - piece
    cols = np.stack([pieces[0], pieces[0], pieces[1], pieces[1], pieces[2], pieces[2]], axis=1)
    tab = cols.reshape(KV_GROUPS, Q_PER_GROUP, 6)
    return jnp.asarray(np.pad(tab, ((0, 0), (0, 8 - Q_PER_GROUP), (0, LANES - 6))))


def _nsa_attention(proj, kcmp, vcmp, gates, batch, seq):
    n = batch * seq
    nq = seq // Q_TILE
    n_cmp = (seq - CMP_BLOCK) // CMP_STRIDE + 1
    n_sel = seq // SEL_BLOCK
    assert seq // CMP_STRIDE == LANES and n_sel <= LANES and Q_TILE == KV_TILE
    assert WINDOW % KV_TILE == 0 and n_sel <= SEL_TOP_N * 2
    kv_heads = [N_HEADS + 2 * KV_GROUPS + k * KV_GROUPS for k in range(4)]
    gates_g = gates[:, :N_GATES].reshape(n, KV_GROUPS, 3 * Q_PER_GROUP).transpose(1, 0, 2)
    gates_g = jnp.pad(gates_g, ((0, 0), (0, 0), (0, LANES - 3 * Q_PER_GROUP)))

    def kv_spec(h0):
        return pl.BlockSpec((None, seq, HEAD_DIM), lambda b, g, i: (h0 + g, b, 0))

    def const_spec(shape):
        return pl.BlockSpec(shape, lambda b, g, i: (0,) * len(shape))

    cmp_spec = pl.BlockSpec((None, None, LANES, HEAD_DIM), lambda b, g, i: (b, g, 0, 0))
    rows = Q_PER_GROUP * Q_TILE
    return pl.pallas_call(
        functools.partial(_nsa_attn_kernel, n_cmp=n_cmp, n_sel=n_sel),
        grid=(batch, KV_GROUPS, nq),
        in_specs=[
            pl.BlockSpec((Q_PER_GROUP, Q_TILE, HEAD_DIM), lambda b, g, i: (g, b * nq + i, 0)),
            pl.BlockSpec((None, 8, LANES), lambda b, g, i: (g, 0, 0)),
            cmp_spec, const_spec((LANES, LANES)), cmp_spec,
            kv_spec(kv_heads[0]), kv_spec(kv_heads[1]), kv_spec(kv_heads[2]), kv_spec(kv_heads[3]),
            const_spec((seq, LANES)),
            pl.BlockSpec((None, Q_TILE, LANES), lambda b, g, i: (g, b * nq + i, 0)),
            const_spec((n_sel, LANES)),
            const_spec((seq // KV_TILE, LANES, KV_TILE)),
        ],
        out_specs=pl.BlockSpec((Q_TILE, Q_PER_GROUP * HEAD_DIM), lambda b, g, i: (b * nq + i, g)),
        out_shape=jax.ShapeDtypeStruct((n, D_MODEL), BF16),
        scratch_shapes=[pltpu.VMEM((rows, LANES), F32), pltpu.VMEM((rows, 2 * HEAD_DIM), F32)],
        compiler_params=_params(("parallel", "parallel", "arbitrary")), name="nsa_attention",
    )(proj, _slope_bias_dims(), kcmp,
      _position_bias_dims(np.arange(LANES) * CMP_STRIDE + (CMP_BLOCK - 1)), vcmp,
      proj, proj, proj, proj, _position_bias_dims(np.arange(seq)), gates_g,
      _cmp_to_sel_overlap(n_cmp, n_sel), _sel_block_expander(seq))


def _fox_attn_kernel(q_ref, qx_ref, k_ref, kx_ref, v_ref, o_ref, m_scr, acc_scr):
    i = pl.program_id(2)
    q_aug = _augment_queries(q_ref, qx_ref)
    _flash_init(m_scr, acc_scr)

    def body(kb, carry):
        k_aug, v_aug = _kv_tile(k_ref, kx_ref, v_ref, pl.multiple_of(kb * KV_TILE, KV_TILE))
        _flash_step(q_aug, k_aug, v_aug, None, m_scr, acc_scr)
        return carry
    lax.fori_loop(0, i, body, 0)
    k_aug, v_aug = _kv_tile(k_ref, kx_ref, v_ref, pl.multiple_of(i * KV_TILE, KV_TILE))
    causal = (lax.broadcasted_iota(I32, (Q_TILE, KV_TILE), 0)
              >= lax.broadcasted_iota(I32, (Q_TILE, KV_TILE), 1))
    _flash_step(q_aug, k_aug, v_aug, causal, m_scr, acc_scr)
    out = _flash_out(acc_scr)
    for r in range(Q_PER_GROUP):
        o_ref[:, r * HEAD_DIM:(r + 1) * HEAD_DIM] = out[r * Q_TILE:(r + 1) * Q_TILE].astype(o_ref.dtype)


def _fox_attention(q_hm, kv_hm, kx, batch, seq):
    n = batch * seq
    nq = seq // Q_TILE
    assert Q_TILE == KV_TILE
    qx = np.zeros((8, LANES), np.float32)
    for r in range(Q_PER_GROUP):
        qx[r, 3 * r:3 * r + 3] = -1.0
    rows = Q_PER_GROUP * Q_TILE
    return pl.pallas_call(
        _fox_attn_kernel, grid=(batch, KV_GROUPS, nq),
        in_specs=[
            pl.BlockSpec((Q_PER_GROUP, Q_TILE, HEAD_DIM), lambda b, g, i: (g, b * nq + i, 0)),
            pl.BlockSpec((8, LANES), lambda b, g, i: (0, 0)),
            pl.BlockSpec((None, seq, HEAD_DIM), lambda b, g, i: (g, b, 0)),
            pl.BlockSpec((None, None, seq, LANES), lambda b, g, i: (b, g, 0, 0)),
            pl.BlockSpec((None, seq, HEAD_DIM), lambda b, g, i: (KV_GROUPS + g, b, 0)),
        ],
        out_specs=pl.BlockSpec((Q_TILE, Q_PER_GROUP * HEAD_DIM), lambda b, g, i: (b * nq + i, g)),
        out_shape=jax.ShapeDtypeStruct((n, D_MODEL), BF16),
        scratch_shapes=[pltpu.VMEM((rows, LANES), F32), pltpu.VMEM((rows, 2 * HEAD_DIM), F32)],
        compiler_params=_params(("parallel", "parallel", "arbitrary")), name="fox_attention",
    )(q_hm, jnp.asarray(qx), kv_hm, kx, kv_hm)


def _cumsum_kernel(x_ref, tri_ref, o_ref, carry_scr):
    @pl.when(pl.program_id(1) == 0)
    def _():
        carry_scr[...] = jnp.zeros(carry_scr.shape, F32)
    tri = tri_ref[...]
    c = sum(jnp.dot(tri, piece, preferred_element_type=F32) for piece in _split3(x_ref[...]))
    c = c + carry_scr[...]
    o_ref[...] = c
    carry_scr[...] = c[SEQ_TILE - 1:SEQ_TILE, :]


def _cumsum_seq(x2, batch, seq):
    nt = seq // SEQ_TILE
    tri = jnp.asarray(np.tril(np.ones((SEQ_TILE, SEQ_TILE))), BF16)
    return pl.pallas_call(
        _cumsum_kernel, grid=(batch, nt),
        in_specs=[pl.BlockSpec((SEQ_TILE, LANES), lambda b, t: (b * nt + t, 0)),
                  pl.BlockSpec((SEQ_TILE, SEQ_TILE), lambda b, t: (0, 0))],
        out_specs=pl.BlockSpec((SEQ_TILE, LANES), lambda b, t: (b * nt + t, 0)),
        out_shape=jax.ShapeDtypeStruct(x2.shape, F32),
        scratch_shapes=[pltpu.VMEM((1, LANES), F32)],
        compiler_params=_params(("parallel", "arbitrary")), name="cumsum_seq",
    )(x2, tri)


def _proj_residual_kernel(o_ref, w_ref, x_ref, out_ref):
    out_ref[...] = x_ref[...] + jnp.dot(o_ref[...], w_ref[...], preferred_element_type=F32)


def _proj_residual(o2, w, x2):
    n, d = x2.shape
    return pl.pallas_call(
        _proj_residual_kernel, grid=(n // ROW_TILE, d // COL_TILE),
        in_specs=[pl.BlockSpec((ROW_TILE, o2.shape[1]), lambda i, j: (i, 0)),
                  pl.BlockSpec((o2.shape[1], COL_TILE), lambda i, j: (0, j)),
                  pl.BlockSpec((ROW_TILE, COL_TILE), lambda i, j: (i, j))],
        out_specs=pl.BlockSpec((ROW_TILE, COL_TILE), lambda i, j: (i, j)),
        out_shape=jax.ShapeDtypeStruct((n, d), F32),
        compiler_params=_params(("parallel", "parallel")), name="proj_residual",
    )(o2, w.astype(BF16), x2)


def _top2_rows(v):
    idx = lax.broadcasted_iota(I32, v.shape, 0).astype(F32)
    m1 = jnp.max(v, axis=0, keepdims=True)
    i1 = jnp.min(jnp.where(v == m1, idx, EXPERTS_PER_GROUP), axis=0, keepdims=True)
    v2 = jnp.where(idx == i1, -jnp.inf, v)
    m2 = jnp.max(v2, axis=0, keepdims=True)
    i2 = jnp.min(jnp.where(v2 == m2, idx, EXPERTS_PER_GROUP), axis=0, keepdims=True)
    return m1, i1, m2, i2


def _router_kernel(x_ref, g_ref, wr_ref, br_ref, h_ref, e_ref, p_ref):
    x = x_ref[...]
    ms = jnp.mean(x * x, axis=-1, keepdims=True)
    h = (x * lax.rsqrt(ms + RMS_EPS)) * g_ref[...]
    h_ref[...] = h.astype(h_ref.dtype)
    h1, h2, _ = _split3(h)
    w1, w2, _ = _split3(wr_ref[...])
    logits = None
    for a, b in ((w1, h1), (w1, h2), (w2, h1)):
        part = lax.dot_general(a, b, _NT, preferred_element_type=F32)
        logits = part if logits is None else logits + part
    m = jnp.max(logits, axis=0, keepdims=True)
    e = jnp.exp(logits - m)
    probs = e / jnp.sum(e, axis=0, keepdims=True)
    sel = probs + br_ref[...]
    idx = lax.broadcasted_iota(I32, (EXPERTS_PER_GROUP, x.shape[0]), 0).astype(F32)
    best_score = best_e1 = best_e2 = best_p1 = best_p2 = None
    for k in range(N_EXPERT_GROUPS):
        rows = slice(k * EXPERTS_PER_GROUP, (k + 1) * EXPERTS_PER_GROUP)
        m1, i1, m2, i2 = _top2_rows(sel[rows])
        pk = probs[rows]
        p1 = jnp.sum(jnp.where(idx == i1, pk, 0.0), axis=0, keepdims=True)
        p2 = jnp.sum(jnp.where(idx == i2, pk, 0.0), axis=0, keepdims=True)
        score = m1 + m2
        e1 = i1 + k * EXPERTS_PER_GROUP
        e2 = i2 + k * EXPERTS_PER_GROUP
        if k == 0:
            best_score, best_e1, best_e2, best_p1, best_p2 = score, e1, e2, p1, p2
        else:
            take = score > best_score
            best_score = jnp.where(take, score, best_score)
            best_e1 = jnp.where(take, e1, best_e1)
            best_e2 = jnp.where(take, e2, best_e2)
            best_p1 = jnp.where(take, p1, best_p1)
            best_p2 = jnp.where(take, p2, best_p2)
    tot = best_p1 + best_p2
    e_ref[...] = jnp.concatenate([best_e1, best_e2], axis=0).astype(I32)
    p_ref[...] = jnp.concatenate([best_p1 / tot, best_p2 / tot], axis=0)


def _router(x2, gain, w_router, b_router):
    n, d = x2.shape
    tm = SEQ_TILE
    return pl.pallas_call(
        _router_kernel, grid=(n // tm,),
        in_specs=[pl.BlockSpec((tm, d), lambda i: (i, 0)),
                  pl.BlockSpec((1, d), lambda i: (0, 0)),
                  pl.BlockSpec((N_EXPERTS, d), lambda i: (0, 0)),
                  pl.BlockSpec((N_EXPERTS, 1), lambda i: (0, 0))],
        out_specs=[pl.BlockSpec((tm, d), lambda i: (i, 0)),
                   pl.BlockSpec((TOP_K, tm), lambda i: (0, i)),
                   pl.BlockSpec((TOP_K, tm), lambda i: (0, i))],
        out_shape=[jax.ShapeDtypeStruct((n, d), BF16),
                   jax.ShapeDtypeStruct((TOP_K, n), I32),
                   jax.ShapeDtypeStruct((TOP_K, n), F32)],
        compiler_params=_params(("parallel",)), name="router",
    )(x2, gain.reshape(1, d).astype(F32), w_router.T.astype(F32),
      b_router.astype(F32).reshape(N_EXPERTS, 1))


def _expert_kernel(be_ref, nu_ref, x_ref, wg_ref, wu_ref, wd_ref, o_ref):
    i = pl.program_id(0)

    @pl.when(i < nu_ref[0])
    def _():
        x = x_ref[...]
        for c in range(D_EXPERT // MOE_F_CHUNK):
            cols = slice(c * MOE_F_CHUNK, (c + 1) * MOE_F_CHUNK)
            gt = jnp.dot(x, wg_ref[:, cols], preferred_element_type=F32)
            up = jnp.dot(x, wu_ref[:, cols], preferred_element_type=F32)
            act = (gt * jax.nn.sigmoid(gt) * up).astype(BF16)
            y = jnp.dot(act, wd_ref[cols, :], preferred_element_type=F32)
            if c == 0:
                o_ref[...] = y
            else:
                o_ref[...] += y

    @pl.when(i >= nu_ref[0])
    def _():
        o_ref[...] = jnp.zeros(o_ref.shape, o_ref.dtype)


def _expert_mlp(xb, block_e, n_used, w_gate, w_up, w_down, layer):
    rows, d = xb.shape
    n_blocks = rows // MOE_ROWS
    f = w_gate.shape[3]
    grid_spec = pltpu.PrefetchScalarGridSpec(
        num_scalar_prefetch=2, grid=(n_blocks,),
        in_specs=[pl.BlockSpec((MOE_ROWS, d), lambda i, be, nu: (i, 0)),
                  pl.BlockSpec((None, None, d, f), lambda i, be, nu: (layer, be[i], 0, 0)),
                  pl.BlockSpec((None, None, d, f), lambda i, be, nu: (layer, be[i], 0, 0)),
                  pl.BlockSpec((None, None, f, d), lambda i, be, nu: (layer, be[i], 0, 0))],
        out_specs=pl.BlockSpec((MOE_ROWS, d), lambda i, be, nu: (i, 0)),
    )
    return pl.pallas_call(
        _expert_kernel, grid_spec=grid_spec,
        out_shape=jax.ShapeDtypeStruct((rows, d), F32),
        compiler_params=_params(("arbitrary",)), name="expert_mlp",
    )(block_e, n_used, xb, w_gate, w_up, w_down)


def _moe(x2, gain, w_router, b_router, w_gate, w_up, w_down, layer):
    n, d = x2.shape
    h, e_idx, e_w = _router(x2, gain, w_router, b_router)
    flat_e = e_idx.reshape(-1)
    n_asg = n * TOP_K
    onehot = (flat_e[:, None] == jnp.arange(N_EXPERTS, dtype=I32)[None, :]).astype(I32)
    counts = jnp.sum(onehot, axis=0)
    rank = jnp.sum((jnp.cumsum(onehot, axis=0) - onehot) * onehot, axis=1)
    blocks_per_e = (counts + MOE_ROWS - 1) // MOE_ROWS
    blk_end = jnp.cumsum(blocks_per_e)
    row_start = (blk_end - blocks_per_e) * MOE_ROWS
    dest = row_start[flat_e] + rank
    n_blocks = -(-(n_asg + N_EXPERTS * (MOE_ROWS - 1)) // MOE_ROWS)
    slot_tok = jnp.zeros((n_blocks * MOE_ROWS,), I32).at[dest].set(
        jnp.arange(n_asg, dtype=I32) % n)
    n_used = blk_end[-1:].astype(I32)
    blk = jnp.minimum(jnp.arange(n_blocks, dtype=I32), n_used[0] - 1)
    block_e = jnp.sum((blk_end[None, :] <= blk[:, None]).astype(I32), axis=1)
    block_e = jnp.minimum(block_e, N_EXPERTS - 1)
    xb = h[slot_tok]
    yb = _expert_mlp(xb, block_e, n_used, w_gate, w_up, w_down, layer)
    out = x2
    for k in range(TOP_K):
        out = out + yb[dest[k * n:(k + 1) * n]] * e_w[k][:, None]
    return out


def _rmsnorm_kernel(x_ref, g_ref, o_ref):
    x = x_ref[...]
    ms = jnp.mean(x * x, axis=-1, keepdims=True)
    o_ref[...] = (x * lax.rsqrt(ms + RMS_EPS)) * g_ref[...]


def _rmsnorm(x2, gain):
    n, d = x2.shape
    tm = NORM_CHUNK * 2
    return pl.pallas_call(
        _rmsnorm_kernel, grid=(n // tm,),
        in_specs=[pl.BlockSpec((tm, d), lambda i: (i, 0)), pl.BlockSpec((1, d), lambda i: (0, 0))],
        out_specs=pl.BlockSpec((tm, d), lambda i: (i, 0)),
        out_shape=jax.ShapeDtypeStruct((n, d), F32),
        compiler_params=_params(("parallel",)), name="final_norm",
    )(x2, gain.reshape(1, d).astype(F32))


def _nsa_layer(x2, gain, w_in, pe_k, pe_v, w1_k, w2_k, w1_v, w2_v, w_out, batch, seq):
    main_cols = D_MODEL + 6 * KV_GROUPS * HEAD_DIM
    proj, gates = _norm_proj(
        x2, gain, w_in[:, :main_cols], n_scaled_cols=D_MODEL, scale=HEAD_DIM ** -0.5,
        wx=w_in[:, main_cols:], bx=jnp.zeros((N_GATES,), F32), extra="sigmoid")
    kcmp = _compress(proj, N_HEADS, pe_k, w1_k, w2_k, batch, seq)
    vcmp = _compress(proj, N_HEADS + KV_GROUPS, pe_v, w1_v, w2_v, batch, seq)
    o = _nsa_attention(proj, kcmp, vcmp, gates, batch, seq)
    return _proj_residual(o, w_out, x2)


def _shared_kv(x2, gain, w_kv, b_f, batch, seq):
    kv_cols = 2 * KV_GROUPS * HEAD_DIM
    kv_hm, logf = _norm_proj(x2, gain, w_kv[:, :kv_cols], wx=w_kv[:, kv_cols:], bx=b_f,
                             extra="log_sigmoid")
    c_tok = _cumsum_seq(logf, batch, seq)
    pieces = jnp.stack(_split3_by_truncation(c_tok[:, :N_HEADS]), axis=-1)
    kx = pieces.reshape(batch, seq, KV_GROUPS, 3 * Q_PER_GROUP).transpose(0, 2, 1, 3)
    kx = jnp.pad(kx, ((0, 0), (0, 0), (0, 0), (0, LANES - 3 * Q_PER_GROUP)))
    return kv_hm, kx


def _fox_layer(x2, gain, w_q, w_out, shared, batch, seq):
    kv_hm, kx = shared
    q_hm = _norm_proj(x2, gain, w_q, n_scaled_cols=D_MODEL, scale=HEAD_DIM ** -0.5)
    o = _fox_attention(q_hm, kv_hm, kx, batch, seq)
    return _proj_residual(o, w_out, x2)


def kernel(x, nsa_w_in, nsa_pe_k, nsa_pe_v, nsa_cmp_w1_k, nsa_cmp_w2_k, nsa_cmp_w1_v, nsa_cmp_w2_v, nsa_w_out, shared_norm, shared_w_kv, shared_b_f, fox_w_q, fox_w_out, attn_norm, ffn_norm, router_w, router_b, moe_w_gate, moe_w_up, moe_w_down, final_norm):
    batch, seq, d = x.shape
    depth = attn_norm.shape[0]
    n_a = nsa_w_in.shape[0]
    x2 = x.reshape(batch * seq, d)
    w_gate, w_up, w_down = (w.astype(BF16) for w in (moe_w_gate, moe_w_up, moe_w_down))
    shared = None
    for layer in range(depth):
        if layer < n_a:
            a = layer
            x2 = _nsa_layer(x2, attn_norm[layer], nsa_w_in[a], nsa_pe_k[a], nsa_pe_v[a],
                            nsa_cmp_w1_k[a], nsa_cmp_w2_k[a], nsa_cmp_w1_v[a], nsa_cmp_w2_v[a],
                            nsa_w_out[a], batch, seq)
        else:
            b = layer - n_a
            x2 = _fox_layer(x2, attn_norm[layer], fox_w_q[b], fox_w_out[b], shared, batch, seq)
        x2 = _moe(x2, ffn_norm[layer], router_w, router_b, w_gate, w_up, w_down, layer)
        if layer == n_a - 1:
            shared = _shared_kv(x2, shared_norm, shared_w_kv, shared_b_f, batch, seq)
    return _rmsnorm(x2, final_norm).reshape(batch, seq, d)
```

```python
import functools

import numpy as np
import jax
import jax.numpy as jnp
from jax import lax
from jax.experimental import pallas as pl
from jax.experimental.pallas import tpu as pltpu

F32 = jnp.float32
BF16 = jnp.bfloat16
I32 = jnp.int32

D_MODEL = 2048
N_HEADS = 16
HEAD_DIM = 128
KV_GROUPS = 4
Q_PER_GROUP = N_HEADS // KV_GROUPS
CMP_BLOCK = 32
CMP_STRIDE = 16
CMP_HIDDEN = 256
SEL_BLOCK = 64
SEL_TOP_N = 16
WINDOW = 512
N_EXPERTS = 32
N_EXPERT_GROUPS = 4
EXPERTS_PER_GROUP = N_EXPERTS // N_EXPERT_GROUPS
TOP_K = 2
D_EXPERT = 1024
RMS_EPS = 1e-6
N_GATES = 3 * N_HEADS

LANES = 128
VMEM_LIMIT = 48 * 1024 * 1024
ROW_TILE = 1024
COL_TILE = 1024
SEQ_TILE = 512
NORM_CHUNK = 128
Q_TILE = 256
KV_TILE = 256
MOE_ROWS = 256
MOE_F_CHUNK = 512
MASK_VALUE = -1e30

_NT = (((1,), (1,)), ((), ()))


def _params(sem):
    return pltpu.CompilerParams(dimension_semantics=sem, vmem_limit_bytes=VMEM_LIMIT)


def _split3(x):
    a = x.astype(BF16)
    r = x - a.astype(F32)
    b = r.astype(BF16)
    c = (r - b.astype(F32)).astype(BF16)
    return a, b, c


def _split3_by_truncation(x):
    def top(v):
        bits = lax.bitcast_convert_type(v, jnp.uint32) & jnp.uint32(0xFFFF0000)
        return lax.bitcast_convert_type(bits, F32)
    a = top(x)
    b = top(x - a)
    c = x - a - b
    return a.astype(BF16), b.astype(BF16), c.astype(BF16)


def _act(kind, v):
    if kind == "sigmoid":
        return jax.nn.sigmoid(v)
    if kind == "log_sigmoid":
        return jnp.minimum(v, 0.0) - jnp.log1p(jnp.exp(-jnp.abs(v)))
    raise ValueError(kind)


def _norm_proj_kernel(*refs, n_scaled, scale, extra):
    if extra:
        x_ref, g_ref, w_ref, wx_ref, bx_ref, o_ref, ox_ref, h_scr = refs
    else:
        x_ref, g_ref, w_ref, o_ref, h_scr = refs
    j = pl.program_id(1)
    tm = x_ref.shape[0]

    @pl.when(j == 0)
    def _():
        def body(c, carry):
            r0 = pl.multiple_of(c * NORM_CHUNK, NORM_CHUNK)
            x = x_ref[pl.ds(r0, NORM_CHUNK), :]
            ms = jnp.mean(x * x, axis=-1, keepdims=True)
            hb = ((x * lax.rsqrt(ms + RMS_EPS)) * g_ref[...]).astype(BF16)
            h_scr[pl.ds(r0, NORM_CHUNK), :] = hb
            if extra:
                e = jnp.dot(hb, wx_ref[...], preferred_element_type=F32) + bx_ref[...]
                ox_ref[pl.ds(r0, NORM_CHUNK), :] = _act(extra, e)
            return carry
        lax.fori_loop(0, tm // NORM_CHUNK, body, 0)

    acc = jnp.dot(h_scr[...], w_ref[...], preferred_element_type=F32)
    if n_scaled:
        acc = acc * jnp.where(j < n_scaled, jnp.float32(scale), jnp.float32(1.0))
    for c in range(o_ref.shape[0]):
        o_ref[c] = acc[:, c * LANES:(c + 1) * LANES].astype(o_ref.dtype)


def _norm_proj(x2, gain, w, *, n_scaled_cols=0, scale=1.0, wx=None, bx=None, extra=None):
    n, d = x2.shape
    cols = w.shape[1]
    assert n % ROW_TILE == 0 and cols % COL_TILE == 0
    grid = (n // ROW_TILE, cols // COL_TILE)
    in_specs = [
        pl.BlockSpec((ROW_TILE, d), lambda i, j: (i, 0)),
        pl.BlockSpec((1, d), lambda i, j: (0, 0)),
        pl.BlockSpec((d, COL_TILE), lambda i, j: (0, j)),
    ]
    args = [x2, gain.reshape(1, d).astype(F32), w.astype(BF16)]
    out_shape = [jax.ShapeDtypeStruct((cols // LANES, n, LANES), BF16)]
    out_specs = [pl.BlockSpec((COL_TILE // LANES, ROW_TILE, LANES), lambda i, j: (j, i, 0))]
    if extra:
        in_specs += [pl.BlockSpec((d, LANES), lambda i, j: (0, 0)),
                     pl.BlockSpec((1, LANES), lambda i, j: (0, 0))]
        pad = LANES - wx.shape[1]
        args += [jnp.pad(wx, ((0, 0), (0, pad))).astype(BF16),
                 jnp.pad(bx.astype(F32), (0, pad)).reshape(1, LANES)]
        out_shape.append(jax.ShapeDtypeStruct((n, LANES), F32))
        out_specs.append(pl.BlockSpec((ROW_TILE, LANES), lambda i, j: (i, 0)))
    kern = functools.partial(_norm_proj_kernel, n_scaled=n_scaled_cols // COL_TILE,
                             scale=scale, extra=extra)
    res = pl.pallas_call(
        kern, grid=grid, in_specs=in_specs, out_specs=out_specs, out_shape=out_shape,
        scratch_shapes=[pltpu.VMEM((ROW_TILE, d), BF16)],
        compiler_params=_params(("parallel", "arbitrary")), name="norm_proj",
    )(*args)
    return res if extra else res[0]


def _gelu_tanh(v):
    c = np.sqrt(2.0 / np.pi).astype(np.float32)
    return 0.5 * v * (1.0 + jnp.tanh(c * (v + 0.044715 * (v * v * v))))


def _compress_kernel(c_ref, pe_ref, w1_ref, w2_ref, o_ref):
    half = w1_ref.shape[0] // 2
    ch = c_ref[...].astype(F32)
    top = (ch + pe_ref[0:1, :]).astype(BF16)
    bot = (ch + pe_ref[1:2, :]).astype(BF16)
    a = jnp.dot(top, w1_ref[0:half, :], preferred_element_type=F32)
    b = jnp.dot(bot, w1_ref[half:, :], preferred_element_type=F32)
    n = a.shape[0]
    b_next = jnp.concatenate([b[1:, :], jnp.zeros((1, b.shape[1]), F32)], axis=0)
    hid = _gelu_tanh(a + b_next).astype(BF16)
    out = jnp.dot(hid, w2_ref[...], preferred_element_type=F32)
    row = lax.broadcasted_iota(I32, out.shape, 0)
    o_ref[...] = jnp.where(row < n - 1, out, 0.0).astype(o_ref.dtype)


def _compress(proj, head0, pe, w1, w2, batch, seq):
    chunks = seq // CMP_STRIDE
    width = CMP_STRIDE * HEAD_DIM
    view = proj[head0:head0 + KV_GROUPS].reshape(KV_GROUPS, batch, chunks, width)
    pe2 = pe.astype(F32).reshape(2, width)
    return pl.pallas_call(
        _compress_kernel, grid=(batch, KV_GROUPS),
        in_specs=[
            pl.BlockSpec((None, None, chunks, width), lambda b, g: (g, b, 0, 0)),
            pl.BlockSpec((2, width), lambda b, g: (0, 0)),
            pl.BlockSpec((2 * width, CMP_HIDDEN), lambda b, g: (0, 0)),
            pl.BlockSpec((CMP_HIDDEN, HEAD_DIM), lambda b, g: (0, 0)),
        ],
        out_specs=pl.BlockSpec((None, None, chunks, HEAD_DIM), lambda b, g: (b, g, 0, 0)),
        out_shape=jax.ShapeDtypeStruct((batch, KV_GROUPS, chunks, HEAD_DIM), BF16),
        compiler_params=_params(("parallel", "parallel")), name="nsa_compress",
    )(view, pe2, w1.astype(BF16), w2.astype(BF16))


def _flash_init(m_scr, acc_scr):
    m_scr[...] = jnp.full(m_scr.shape, MASK_VALUE, F32)
    acc_scr[...] = jnp.zeros(acc_scr.shape, F32)


def _twice(v):
    return jnp.concatenate([v, v], axis=1)


def _flash_step(q_aug, k_aug, v_aug, valid, m_scr, acc_scr, by_head=True):
    n_slabs = Q_PER_GROUP if (valid is not None and by_head) else 1
    slab = q_aug.shape[0] // n_slabs
    for r in range(n_slabs):
        rows = slice(r * slab, (r + 1) * slab)
        s = lax.dot_general(q_aug[rows], k_aug, _NT, preferred_element_type=F32)
        if valid is not None and by_head:
            s = jnp.where(valid, s, MASK_VALUE)
        elif valid is not None:
            s = jnp.concatenate(
                [jnp.where(valid, s[h * Q_TILE:(h + 1) * Q_TILE], MASK_VALUE)
                 for h in range(Q_PER_GROUP)], axis=0)
        m_prev = m_scr[rows, :]
        m_new = jnp.maximum(m_prev, jnp.max(s, axis=-1, keepdims=True))
        alpha = jnp.exp(m_prev - m_new)
        p = jnp.exp(s - _twice(m_new))
        acc_scr[rows, :] = _twice(alpha) * acc_scr[rows, :] + jnp.dot(
            p.astype(BF16), v_aug, preferred_element_type=F32)
        m_scr[rows, :] = m_new


def _flash_out(acc_scr):
    acc = acc_scr[...]
    return acc[:, :HEAD_DIM] / acc[:, HEAD_DIM:]


def _augment_queries(q_ref, qx_ref):
    q = q_ref[...].reshape(Q_PER_GROUP * Q_TILE, HEAD_DIM)
    qx_tab = qx_ref[...]
    qx = jnp.concatenate(
        [jnp.broadcast_to(qx_tab[r:r + 1, :], (Q_TILE, LANES)) for r in range(Q_PER_GROUP)], axis=0)
    return jnp.concatenate([q, qx.astype(BF16)], axis=1)


def _kv_tile(k_ref, kx_ref, v_ref, k0):
    k_aug = jnp.concatenate([k_ref[pl.ds(k0, KV_TILE), :], kx_ref[pl.ds(k0, KV_TILE), :]], axis=1)
    v_aug = jnp.concatenate([v_ref[pl.ds(k0, KV_TILE), :], jnp.ones((KV_TILE, HEAD_DIM), BF16)],
                            axis=1)
    return k_aug, v_aug


def _nsa_attn_kernel(q_ref, qx_ref, kc_ref, kcx_ref, vc_ref, ks_ref, vs_ref, kw_ref, vw_ref,
                     kx_ref, gate_ref, ovl_ref, exp_ref, o_ref, m_scr, acc_scr, *, n_cmp, n_sel):
    i = pl.program_id(2)
    q0 = i * Q_TILE
    q_aug = _augment_queries(q_ref, qx_ref)

    kc_aug = jnp.concatenate([kc_ref[...], kcx_ref[...]], axis=1)
    s = lax.dot_general(q_aug, kc_aug, _NT, preferred_element_type=F32)
    t_loc = lax.broadcasted_iota(I32, (Q_TILE, LANES), 0)
    n_loc = lax.broadcasted_iota(I32, (Q_TILE, LANES), 1)
    block_end = n_loc * CMP_STRIDE + (CMP_BLOCK - 1)
    valid_c = jnp.logical_and(q0 + t_loc >= block_end, n_loc < n_cmp)
    probs = []
    for r in range(Q_PER_GROUP):
        s_r = jnp.where(valid_c, s[r * Q_TILE:(r + 1) * Q_TILE, :], -jnp.inf)
        m = jnp.max(s_r, axis=-1, keepdims=True)
        m = jnp.where(m == -jnp.inf, 0.0, m)
        e = jnp.exp(s_r - m)
        d = jnp.sum(e, axis=-1, keepdims=True)
        probs.append(e / jnp.where(d > 0, d, 1.0))
    p_all = jnp.concatenate(probs, axis=0)
    o_cmp = jnp.dot(p_all.astype(BF16), vc_ref[...], preferred_element_type=F32)

    p_sum_t = (probs[0] + probs[1] + probs[2] + probs[3]).T
    imp = sum(jnp.dot(ovl_ref[...], piece, preferred_element_type=F32)
              for piece in _split3(p_sum_t))
    j_idx = lax.broadcasted_iota(I32, (n_sel, Q_TILE), 0)
    cur = (q0 + lax.broadcasted_iota(I32, (n_sel, Q_TILE), 1)) // SEL_BLOCK
    forced = jnp.logical_or(j_idx == 0, jnp.logical_or(j_idx == cur, j_idx == cur - 1))
    imp = jnp.where(j_idx > cur, -jnp.inf, jnp.where(forced, jnp.inf, imp))
    rank = jnp.zeros((n_sel, Q_TILE), F32)
    for k in range(n_sel):
        row = imp[k:k + 1, :]
        ahead = jnp.logical_or(row > imp, jnp.logical_and(row == imp, j_idx > k))
        rank = rank + jnp.where(ahead, 1.0, 0.0)
    chosen_t = jnp.where(rank < SEL_TOP_N, 1.0, 0.0)
    chosen = jnp.concatenate([chosen_t, jnp.zeros((LANES - n_sel, Q_TILE), F32)], axis=0)
    chosen = chosen.T.astype(BF16)

    q_loc = lax.broadcasted_iota(I32, (Q_TILE, KV_TILE), 0)
    k_loc = lax.broadcasted_iota(I32, (Q_TILE, KV_TILE), 1)
    causal = q_loc >= k_loc
    diag0 = pl.multiple_of(i * KV_TILE, KV_TILE)

    def picked(kb):
        return jnp.dot(chosen, exp_ref[kb], preferred_element_type=F32) > 0.5

    _flash_init(m_scr, acc_scr)

    def sel_body(kb, carry):
        k_aug, v_aug = _kv_tile(ks_ref, kx_ref, vs_ref, pl.multiple_of(kb * KV_TILE, KV_TILE))
        _flash_step(q_aug, k_aug, v_aug, picked(kb), m_scr, acc_scr)
        return carry
    lax.fori_loop(0, i, sel_body, 0)
    k_aug, v_aug = _kv_tile(ks_ref, kx_ref, vs_ref, diag0)
    _flash_step(q_aug, k_aug, v_aug, jnp.logical_and(picked(i), causal), m_scr, acc_scr)
    o_slc = _flash_out(acc_scr)

    _flash_init(m_scr, acc_scr)
    n_back = WINDOW // KV_TILE

    @pl.when(i >= n_back)
    def _():
        k_aug, v_aug = _kv_tile(kw_ref, kx_ref, vw_ref,
                                pl.multiple_of((i - n_back) * KV_TILE, KV_TILE))
        _flash_step(q_aug, k_aug, v_aug, k_loc > q_loc, m_scr, acc_scr)

    def win_body(kb, carry):
        k_aug, v_aug = _kv_tile(kw_ref, kx_ref, vw_ref, pl.multiple_of(kb * KV_TILE, KV_TILE))
        _flash_step(q_aug, k_aug, v_aug, None, m_scr, acc_scr)
        return carry
    lax.fori_loop(jnp.maximum(i - n_back + 1, 0), i, win_body, 0)
    k_aug, v_aug = _kv_tile(kw_ref, kx_ref, vw_ref, diag0)
    _flash_step(q_aug, k_aug, v_aug, causal, m_scr, acc_scr)
    o_win = _flash_out(acc_scr)

    gates = gate_ref[...]
    for r in range(Q_PER_GROUP):
        sl = slice(r * Q_TILE, (r + 1) * Q_TILE)
        g_cmp, g_slc, g_win = (gates[:, 3 * r + b:3 * r + b + 1] for b in range(3))
        o_r = g_cmp * o_cmp[sl] + g_slc * o_slc[sl] + g_win * o_win[sl]
        o_ref[:, r * HEAD_DIM:(r + 1) * HEAD_DIM] = o_r.astype(o_ref.dtype)


def _cmp_to_sel_overlap(n_cmp, n_sel):
    cs = np.arange(LANES)[None, :] * CMP_STRIDE
    ce = cs + CMP_BLOCK
    ss = np.arange(n_sel)[:, None] * SEL_BLOCK
    se = ss + SEL_BLOCK
    ov = np.clip(np.minimum(ce, se) - np.maximum(cs, ss), 0, None) / CMP_BLOCK
    ov = ov * (np.arange(LANES)[None, :] < n_cmp)
    return jnp.asarray(ov, BF16)


def _sel_block_expander(seq):
    n_kv = seq // KV_TILE
    blk = (np.arange(n_kv)[:, None, None] * KV_TILE + np.arange(KV_TILE)[None, None, :]) // SEL_BLOCK
    return jnp.asarray(blk == np.arange(LANES)[None, :, None], BF16)


def _alibi_slopes():
    sl = 2.0 ** (-8.0 * np.arange(1, N_HEADS + 1) / N_HEADS)
    return jnp.asarray(sl, F32)


def _position_bias_dims(pos):
    hi = (pos // 256) * 256
    lo = pos % 256
    cols = np.stack([hi, lo] * 3, axis=1).astype(np.float32)
    return jnp.asarray(np.pad(cols, ((0, 0), (0, LANES - cols.shape[1]))), BF16)


def _slope_bias_dims():
    rest = (2.0 ** (-8.0 * np.arange(1, N_HEADS + 1) / N_HEADS)).astype(np.float32)
    pieces = []
    for _ in range(3):
        piece = rest.astype(BF16).astype(np.float32)
        pieces.append(piece)
        rest = rest - piece
    cols = np.stack([pieces[0], pieces[0], pieces[1], pieces[1], pieces[2], pieces[2]], axis=1)
    tab = cols.reshape(KV_GROUPS, Q_PER_GROUP, 6)
    return jnp.asarray(np.pad(tab, ((0, 0), (0, 8 - Q_PER_GROUP), (0, LANES - 6))))


def _nsa_attention(proj, kcmp, vcmp, gates, batch, seq):
    n = batch * seq
    nq = seq // Q_TILE
    n_cmp = (seq - CMP_BLOCK) // CMP_STRIDE + 1
    n_sel = seq // SEL_BLOCK
    assert seq // CMP_STRIDE == LANES and n_sel <= LANES and Q_TILE == KV_TILE
    assert WINDOW % KV_TILE == 0 and n_sel <= SEL_TOP_N * 2
    kv_heads = [N_HEADS + 2 * KV_GROUPS + k * KV_GROUPS for k in range(4)]
    gates_g = gates[:, :N_GATES].reshape(n, KV_GROUPS, 3 * Q_PER_GROUP).transpose(1, 0, 2)
    gates_g = jnp.pad(gates_g, ((0, 0), (0, 0), (0, LANES - 3 * Q_PER_GROUP)))

    def kv_spec(h0):
        return pl.BlockSpec((None, seq, HEAD_DIM), lambda b, g, i: (h0 + g, b, 0))

    def const_spec(shape):
        return pl.BlockSpec(shape, lambda b, g, i: (0,) * len(shape))

    cmp_spec = pl.BlockSpec((None, None, LANES, HEAD_DIM), lambda b, g, i: (b, g, 0, 0))
    rows = Q_PER_GROUP * Q_TILE
    return pl.pallas_call(
        functools.partial(_nsa_attn_kernel, n_cmp=n_cmp, n_sel=n_sel),
        grid=(batch, KV_GROUPS, nq),
        in_specs=[
            pl.BlockSpec((Q_PER_GROUP, Q_TILE, HEAD_DIM), lambda b, g, i: (g, b * nq + i, 0)),
            pl.BlockSpec((None, 8, LANES), lambda b, g, i: (g, 0, 0)),
            cmp_spec, const_spec((LANES, LANES)), cmp_spec,
            kv_spec(kv_heads[0]), kv_spec(kv_heads[1]), kv_spec(kv_heads[2]), kv_spec(kv_heads[3]),
            const_spec((seq, LANES)),
            pl.BlockSpec((None, Q_TILE, LANES), lambda b, g, i: (g, b * nq + i, 0)),
            const_spec((n_sel, LANES)),
            const_spec((seq // KV_TILE, LANES, KV_TILE)),
        ],
        out_specs=pl.BlockSpec((Q_TILE, Q_PER_GROUP * HEAD_DIM), lambda b, g, i: (b * nq + i, g)),
        out_shape=jax.ShapeDtypeStruct((n, D_MODEL), BF16),
        scratch_shapes=[pltpu.VMEM((rows, LANES), F32), pltpu.VMEM((rows, 2 * HEAD_DIM), F32)],
        compiler_params=_params(("parallel", "parallel", "arbitrary")), name="nsa_attention",
    )(proj, _slope_bias_dims(), kcmp,
      _position_bias_dims(np.arange(LANES) * CMP_STRIDE + (CMP_BLOCK - 1)), vcmp,
      proj, proj, proj, proj, _position_bias_dims(np.arange(seq)), gates_g,
      _cmp_to_sel_overlap(n_cmp, n_sel), _sel_block_expander(seq))


def _fox_attn_kernel(q_ref, qx_ref, k_ref, kx_ref, v_ref, o_ref, m_scr, acc_scr):
    i = pl.program_id(2)
    q_aug = _augment_queries(q_ref, qx_ref)
    _flash_init(m_scr, acc_scr)

    def body(kb, carry):
        k_aug, v_aug = _kv_tile(k_ref, kx_ref, v_ref, pl.multiple_of(kb * KV_TILE, KV_TILE))
        _flash_step(q_aug, k_aug, v_aug, None, m_scr, acc_scr)
        return carry
    lax.fori_loop(0, i, body, 0)
    k_aug, v_aug = _kv_tile(k_ref, kx_ref, v_ref, pl.multiple_of(i * KV_TILE, KV_TILE))
    causal = (lax.broadcasted_iota(I32, (Q_TILE, KV_TILE), 0)
              >= lax.broadcasted_iota(I32, (Q_TILE, KV_TILE), 1))
    _flash_step(q_aug, k_aug, v_aug, causal, m_scr, acc_scr, by_head=False)
    out = _flash_out(acc_scr)
    for r in range(Q_PER_GROUP):
        o_ref[:, r * HEAD_DIM:(r + 1) * HEAD_DIM] = out[r * Q_TILE:(r + 1) * Q_TILE].astype(o_ref.dtype)


def _fox_attention(q_hm, kv_hm, kx, batch, seq):
    n = batch * seq
    nq = seq // Q_TILE
    assert Q_TILE == KV_TILE
    qx = np.zeros((8, LANES), np.float32)
    for r in range(Q_PER_GROUP):
        qx[r, 3 * r:3 * r + 3] = -1.0
    rows = Q_PER_GROUP * Q_TILE
    return pl.pallas_call(
        _fox_attn_kernel, grid=(batch, KV_GROUPS, nq),
        in_specs=[
            pl.BlockSpec((Q_PER_GROUP, Q_TILE, HEAD_DIM), lambda b, g, i: (g, b * nq + i, 0)),
            pl.BlockSpec((8, LANES), lambda b, g, i: (0, 0)),
            pl.BlockSpec((None, seq, HEAD_DIM), lambda b, g, i: (g, b, 0)),
            pl.BlockSpec((None, None, seq, LANES), lambda b, g, i: (b, g, 0, 0)),
            pl.BlockSpec((None, seq, HEAD_DIM), lambda b, g, i: (KV_GROUPS + g, b, 0)),
        ],
        out_specs=pl.BlockSpec((Q_TILE, Q_PER_GROUP * HEAD_DIM), lambda b, g, i: (b * nq + i, g)),
        out_shape=jax.ShapeDtypeStruct((n, D_MODEL), BF16),
        scratch_shapes=[pltpu.VMEM((rows, LANES), F32), pltpu.VMEM((rows, 2 * HEAD_DIM), F32)],
        compiler_params=_params(("parallel", "parallel", "arbitrary")), name="fox_attention",
    )(q_hm, jnp.asarray(qx), kv_hm, kx, kv_hm)


def _cumsum_kernel(x_ref, tri_ref, o_ref, carry_scr):
    @pl.when(pl.program_id(1) == 0)
    def _():
        carry_scr[...] = jnp.zeros(carry_scr.shape, F32)
    tri = tri_ref[...]
    c = sum(jnp.dot(tri, piece, preferred_element_type=F32) for piece in _split3(x_ref[...]))
    c = c + carry_scr[...]
    o_ref[...] = c
    carry_scr[...] = c[SEQ_TILE - 1:SEQ_TILE, :]


def _cumsum_seq(x2, batch, seq):
    nt = seq // SEQ_TILE
    tri = jnp.asarray(np.tril(np.ones((SEQ_TILE, SEQ_TILE))), BF16)
    return pl.pallas_call(
        _cumsum_kernel, grid=(batch, nt),
        in_specs=[pl.BlockSpec((SEQ_TILE, LANES), lambda b, t: (b * nt + t, 0)),
                  pl.BlockSpec((SEQ_TILE, SEQ_TILE), lambda b, t: (0, 0))],
        out_specs=pl.BlockSpec((SEQ_TILE, LANES), lambda b, t: (b * nt + t, 0)),
        out_shape=jax.ShapeDtypeStruct(x2.shape, F32),
        scratch_shapes=[pltpu.VMEM((1, LANES), F32)],
        compiler_params=_params(("parallel", "arbitrary")), name="cumsum_seq",
    )(x2, tri)


def _proj_residual_kernel(o_ref, w_ref, x_ref, out_ref):
    out_ref[...] = x_ref[...] + jnp.dot(o_ref[...], w_ref[...], preferred_element_type=F32)


def _proj_residual(o2, w, x2):
    n, d = x2.shape
    return pl.pallas_call(
        _proj_residual_kernel, grid=(n // ROW_TILE, d // COL_TILE),
        in_specs=[pl.BlockSpec((ROW_TILE, o2.shape[1]), lambda i, j: (i, 0)),
                  pl.BlockSpec((o2.shape[1], COL_TILE), lambda i, j: (0, j)),
                  pl.BlockSpec((ROW_TILE, COL_TILE), lambda i, j: (i, j))],
        out_specs=pl.BlockSpec((ROW_TILE, COL_TILE), lambda i, j: (i, j)),
        out_shape=jax.ShapeDtypeStruct((n, d), F32),
        compiler_params=_params(("parallel", "parallel")), name="proj_residual",
    )(o2, w.astype(BF16), x2)


def _top2_rows(v):
    idx = lax.broadcasted_iota(I32, v.shape, 0).astype(F32)
    m1 = jnp.max(v, axis=0, keepdims=True)
    i1 = jnp.min(jnp.where(v == m1, idx, EXPERTS_PER_GROUP), axis=0, keepdims=True)
    v2 = jnp.where(idx == i1, -jnp.inf, v)
    m2 = jnp.max(v2, axis=0, keepdims=True)
    i2 = jnp.min(jnp.where(v2 == m2, idx, EXPERTS_PER_GROUP), axis=0, keepdims=True)
    return m1, i1, m2, i2


def _router_kernel(x_ref, g_ref, wr_ref, br_ref, h_ref, e_ref, p_ref):
    x = x_ref[...]
    ms = jnp.mean(x * x, axis=-1, keepdims=True)
    h = (x * lax.rsqrt(ms + RMS_EPS)) * g_ref[...]
    h_ref[...] = h.astype(h_ref.dtype)
    h1, h2, _ = _split3(h)
    w1, w2, _ = _split3(wr_ref[...])
    logits = None
    for a, b in ((w1, h1), (w1, h2), (w2, h1)):
        part = lax.dot_general(a, b, _NT, preferred_element_type=F32)
        logits = part if logits is None else logits + part
    m = jnp.max(logits, axis=0, keepdims=True)
    e = jnp.exp(logits - m)
    probs = e / jnp.sum(e, axis=0, keepdims=True)
    sel = probs + br_ref[...]
    idx = lax.broadcasted_iota(I32, (EXPERTS_PER_GROUP, x.shape[0]), 0).astype(F32)
    best_score = best_e1 = best_e2 = best_p1 = best_p2 = None
    for k in range(N_EXPERT_GROUPS):
        rows = slice(k * EXPERTS_PER_GROUP, (k + 1) * EXPERTS_PER_GROUP)
        m1, i1, m2, i2 = _top2_rows(sel[rows])
        pk = probs[rows]
        p1 = jnp.sum(jnp.where(idx == i1, pk, 0.0), axis=0, keepdims=True)
        p2 = jnp.sum(jnp.where(idx == i2, pk, 0.0), axis=0, keepdims=True)
        score = m1 + m2
        e1 = i1 + k * EXPERTS_PER_GROUP
        e2 = i2 + k * EXPERTS_PER_GROUP
        if k == 0:
            best_score, best_e1, best_e2, best_p1, best_p2 = score, e1, e2, p1, p2
        else:
            take = score > best_score
            best_score = jnp.where(take, score, best_score)
            best_e1 = jnp.where(take, e1, best_e1)
            best_e2 = jnp.where(take, e2, best_e2)
            best_p1 = jnp.where(take, p1, best_p1)
            best_p2 = jnp.where(take, p2, best_p2)
    tot = best_p1 + best_p2
    e_ref[...] = jnp.concatenate([best_e1, best_e2], axis=0).astype(I32)
    p_ref[...] = jnp.concatenate([best_p1 / tot, best_p2 / tot], axis=0)


def _router(x2, gain, w_router, b_router):
    n, d = x2.shape
    tm = SEQ_TILE
    return pl.pallas_call(
        _router_kernel, grid=(n // tm,),
        in_specs=[pl.BlockSpec((tm, d), lambda i: (i, 0)),
                  pl.BlockSpec((1, d), lambda i: (0, 0)),
                  pl.BlockSpec((N_EXPERTS, d), lambda i: (0, 0)),
                  pl.BlockSpec((N_EXPERTS, 1), lambda i: (0, 0))],
        out_specs=[pl.BlockSpec((tm, d), lambda i: (i, 0)),
                   pl.BlockSpec((TOP_K, tm), lambda i: (0, i)),
                   pl.BlockSpec((TOP_K, tm), lambda i: (0, i))],
        out_shape=[jax.ShapeDtypeStruct((n, d), BF16),
                   jax.ShapeDtypeStruct((TOP_K, n), I32),
                   jax.ShapeDtypeStruct((TOP_K, n), F32)],
        compiler_params=_params(("parallel",)), name="router",
    )(x2, gain.reshape(1, d).astype(F32), w_router.T.astype(F32),
      b_router.astype(F32).reshape(N_EXPERTS, 1))


def _expert_kernel(be_ref, nu_ref, x_ref, wg_ref, wu_ref, wd_ref, o_ref):
    i = pl.program_id(0)

    @pl.when(i < nu_ref[0])
    def _():
        x = x_ref[...]
        for c in range(D_EXPERT // MOE_F_CHUNK):
            cols = slice(c * MOE_F_CHUNK, (c + 1) * MOE_F_CHUNK)
            gt = jnp.dot(x, wg_ref[:, cols], preferred_element_type=F32)
            up = jnp.dot(x, wu_ref[:, cols], preferred_element_type=F32)
            act = (gt * jax.nn.sigmoid(gt) * up).astype(BF16)
            y = jnp.dot(act, wd_ref[cols, :], preferred_element_type=F32)
            if c == 0:
                o_ref[...] = y
            else:
                o_ref[...] += y

    @pl.when(i >= nu_ref[0])
    def _():
        o_ref[...] = jnp.zeros(o_ref.shape, o_ref.dtype)


def _expert_mlp(xb, block_e, n_used, w_gate, w_up, w_down, layer):
    rows, d = xb.shape
    n_blocks = rows // MOE_ROWS
    f = w_gate.shape[3]
    grid_spec = pltpu.PrefetchScalarGridSpec(
        num_scalar_prefetch=2, grid=(n_blocks,),
        in_specs=[pl.BlockSpec((MOE_ROWS, d), lambda i, be, nu: (i, 0)),
                  pl.BlockSpec((None, None, d, f), lambda i, be, nu: (layer, be[i], 0, 0)),
                  pl.BlockSpec((None, None, d, f), lambda i, be, nu: (layer, be[i], 0, 0)),
                  pl.BlockSpec((None, None, f, d), lambda i, be, nu: (layer, be[i], 0, 0))],
        out_specs=pl.BlockSpec((MOE_ROWS, d), lambda i, be, nu: (i, 0)),
    )
    return pl.pallas_call(
        _expert_kernel, grid_spec=grid_spec,
        out_shape=jax.ShapeDtypeStruct((rows, d), F32),
        compiler_params=_params(("arbitrary",)), name="expert_mlp",
    )(block_e, n_used, xb, w_gate, w_up, w_down)


def _moe(x2, gain, w_router, b_router, w_gate, w_up, w_down, layer):
    n, d = x2.shape
    h, e_idx, e_w = _router(x2, gain, w_router, b_router)
    flat_e = e_idx.reshape(-1)
    n_asg = n * TOP_K
    onehot = (flat_e[:, None] == jnp.arange(N_EXPERTS, dtype=I32)[None, :]).astype(I32)
    counts = jnp.sum(onehot, axis=0)
    rank = jnp.sum((jnp.cumsum(onehot, axis=0) - onehot) * onehot, axis=1)
    blocks_per_e = (counts + MOE_ROWS - 1) // MOE_ROWS
    blk_end = jnp.cumsum(blocks_per_e)
    row_start = (blk_end - blocks_per_e) * MOE_ROWS
    dest = row_start[flat_e] + rank
    n_blocks = -(-(n_asg + N_EXPERTS * (MOE_ROWS - 1)) // MOE_ROWS)
    slot_tok = jnp.zeros((n_blocks * MOE_ROWS,), I32).at[dest].set(
        jnp.arange(n_asg, dtype=I32) % n)
    n_used = blk_end[-1:].astype(I32)
    blk = jnp.minimum(jnp.arange(n_blocks, dtype=I32), n_used[0] - 1)
    block_e = jnp.sum((blk_end[None, :] <= blk[:, None]).astype(I32), axis=1)
    block_e = jnp.minimum(block_e, N_EXPERTS - 1)
    xb = h[slot_tok]
    yb = _expert_mlp(xb, block_e, n_used, w_gate, w_up, w_down, layer)
    out = x2
    for k in range(TOP_K):
        out = out + yb[dest[k * n:(k + 1) * n]] * e_w[k][:, None]
    return out


def _rmsnorm_kernel(x_ref, g_ref, o_ref):
    x = x_ref[...]
    ms = jnp.mean(x * x, axis=-1, keepdims=True)
    o_ref[...] = (x * lax.rsqrt(ms + RMS_EPS)) * g_ref[...]


def _rmsnorm(x2, gain):
    n, d = x2.shape
    tm = NORM_CHUNK * 2
    return pl.pallas_call(
        _rmsnorm_kernel, grid=(n // tm,),
        in_specs=[pl.BlockSpec((tm, d), lambda i: (i, 0)), pl.BlockSpec((1, d), lambda i: (0, 0))],
        out_specs=pl.BlockSpec((tm, d), lambda i: (i, 0)),
        out_shape=jax.ShapeDtypeStruct((n, d), F32),
        compiler_params=_params(("parallel",)), name="final_norm",
    )(x2, gain.reshape(1, d).astype(F32))


def _nsa_layer(x2, gain, w_in, pe_k, pe_v, w1_k, w2_k, w1_v, w2_v, w_out, batch, seq):
    main_cols = D_MODEL + 6 * KV_GROUPS * HEAD_DIM
    proj, gates = _norm_proj(
        x2, gain, w_in[:, :main_cols], n_scaled_cols=D_MODEL, scale=HEAD_DIM ** -0.5,
        wx=w_in[:, main_cols:], bx=jnp.zeros((N_GATES,), F32), extra="sigmoid")
    kcmp = _compress(proj, N_HEADS, pe_k, w1_k, w2_k, batch, seq)
    vcmp = _compress(proj, N_HEADS + KV_GROUPS, pe_v, w1_v, w2_v, batch, seq)
    o = _nsa_attention(proj, kcmp, vcmp, gates, batch, seq)
    return _proj_residual(o, w_out, x2)


def _shared_kv(x2, gain, w_kv, b_f, batch, seq):
    kv_cols = 2 * KV_GROUPS * HEAD_DIM
    kv_hm, logf = _norm_proj(x2, gain, w_kv[:, :kv_cols], wx=w_kv[:, kv_cols:], bx=b_f,
                             extra="log_sigmoid")
    c_tok = _cumsum_seq(logf, batch, seq)
    pieces = jnp.stack(_split3_by_truncation(c_tok[:, :N_HEADS]), axis=-1)
    kx = pieces.reshape(batch, seq, KV_GROUPS, 3 * Q_PER_GROUP).transpose(0, 2, 1, 3)
    kx = jnp.pad(kx, ((0, 0), (0, 0), (0, 0), (0, LANES - 3 * Q_PER_GROUP)))
    return kv_hm, kx


def _fox_layer(x2, gain, w_q, w_out, shared, batch, seq):
    kv_hm, kx = shared
    q_hm = _norm_proj(x2, gain, w_q, n_scaled_cols=D_MODEL, scale=HEAD_DIM ** -0.5)
    o = _fox_attention(q_hm, kv_hm, kx, batch, seq)
    return _proj_residual(o, w_out, x2)


def kernel(x, nsa_w_in, nsa_pe_k, nsa_pe_v, nsa_cmp_w1_k, nsa_cmp_w2_k, nsa_cmp_w1_v, nsa_cmp_w2_v, nsa_w_out, shared_norm, shared_w_kv, shared_b_f, fox_w_q, fox_w_out, attn_norm, ffn_norm, router_w, router_b, moe_w_gate, moe_w_up, moe_w_down, final_norm):
    batch, seq, d = x.shape
    depth = attn_norm.shape[0]
    n_a = nsa_w_in.shape[0]
    x2 = x.reshape(batch * seq, d)
    w_gate, w_up, w_down = (w.astype(BF16) for w in (moe_w_gate, moe_w_up, moe_w_down))
    shared = None
    for layer in range(depth):
        if layer < n_a:
            a = layer
            x2 = _nsa_layer(x2, attn_norm[layer], nsa_w_in[a], nsa_pe_k[a], nsa_pe_v[a],
                            nsa_cmp_w1_k[a], nsa_cmp_w2_k[a], nsa_cmp_w1_v[a], nsa_cmp_w2_v[a],
                            nsa_w_out[a], batch, seq)
        else:
            b = layer - n_a
            x2 = _fox_layer(x2, attn_norm[layer], fox_w_q[b], fox_w_out[b], shared, batch, seq)
        x2 = _moe(x2, ffn_norm[layer], router_w, router_b, w_gate, w_up, w_down, layer)
        if layer == n_a - 1:
            shared = _shared_kv(x2, shared_norm, shared_w_kv, shared_b_f, batch, seq)
    return _rmsnorm(x2, final_norm).reshape(batch, seq, d)
```
